```python
import math
import jax, jax.numpy as jnp
from jax import lax
import numpy as np

D_MODEL = 1024
BATCH = 2
SEQ = 8192
DEPTH = 2
DEC_BATCH = 16
DEC_SEQ = 32
PAST_LEN = 4096

CHUNK = 64
N_A = DEPTH // 2
N_B = DEPTH - N_A
CONV_W = 3
D_FF = ((8 * D_MODEL // 3 + 255) // 256) * 256
N_HEADS = 8
HEAD_DIM = D_MODEL // (2 * N_HEADS)
V_DIM = 2 * HEAD_DIM
ATT_DIM = N_HEADS * V_DIM
Q_BLOCK = 128
EPS = 1e-6
NEG = -1e30

kernel_name = "yoco_shortconv_diffattn_stream_step"


def rmsnorm(x, g):
    xf = x.astype(jnp.float32)
    y = xf * lax.rsqrt(jnp.mean(xf * xf, axis=-1, keepdims=True) + EPS)
    return (y * g.astype(jnp.float32)).astype(x.dtype)


def swiglu(x, w_in, w_out):
    g, u = jnp.split(x @ w_in, 2, axis=-1)
    return (jax.nn.silu(g) * u) @ w_out


def short_conv_mixer(x, w_in, conv_w, w_out, conv_state):
    s = x.shape[1]
    b_g, c_g, h = jnp.split(x @ w_in, 3, axis=-1)
    u = c_g * h
    u_ext = jnp.concatenate([conv_state.astype(u.dtype), u], axis=1)
    conv = sum(conv_w[k] * u_ext[:, k:k + s] for k in range(CONV_W))
    return (b_g * conv) @ w_out, u_ext[:, s:]


def diff_attend(q1, q2, k1, k2, v, lam, q_pos, k_pos):
    scale = HEAD_DIM ** -0.5
    mask = (k_pos[None, :] // CHUNK) <= (q_pos[:, None] // CHUNK)

    def probs(q, k):
        sc = jnp.einsum('bqhd,bnhd->bhqn', q.astype(jnp.float32), k.astype(jnp.float32)) * scale
        return jax.nn.softmax(jnp.where(mask, sc, NEG), axis=-1)

    a = probs(q1, k1) - lam * probs(q2, k2)
    return jnp.einsum('bhqn,bnhe->bqhe', a, v.astype(jnp.float32))


def diff_attention(q1, q2, k1, k2, v, lam, q_start):
    b, n_q = q1.shape[0], q1.shape[1]
    k_pos = jnp.arange(k1.shape[1])
    if n_q > Q_BLOCK:
        nb = n_q // Q_BLOCK
        q1b = q1.reshape(b, nb, Q_BLOCK, N_HEADS, HEAD_DIM).transpose(1, 0, 2, 3, 4)
        q2b = q2.reshape(b, nb, Q_BLOCK, N_HEADS, HEAD_DIM).transpose(1, 0, 2, 3, 4)

        def blk(args):
            i, qa, qb = args
            q_pos = q_start + i * Q_BLOCK + jnp.arange(Q_BLOCK)
            return diff_attend(qa, qb, k1, k2, v, lam, q_pos, k_pos)

        out = lax.map(blk, (jnp.arange(nb), q1b, q2b))
        return out.transpose(1, 0, 2, 3, 4).reshape(b, n_q, N_HEADS, V_DIM)
    q_pos = q_start + jnp.arange(n_q)
    return diff_attend(q1, q2, k1, k2, v, lam, q_pos, k_pos)


def diff_attn_mixer(xn, k1, k2, v, w_q, lq1, lk1, lq2, lk2, subln, w_o, lambda_init, q_start):
    b, s, _ = xn.shape
    q = (xn @ w_q).reshape(b, s, N_HEADS, 2, HEAD_DIM)
    lam = (jnp.exp(jnp.sum(lq1.astype(jnp.float32) * lk1.astype(jnp.float32)))
           - jnp.exp(jnp.sum(lq2.astype(jnp.float32) * lk2.astype(jnp.float32))) + lambda_init)
    o = diff_attention(q[..., 0, :], q[..., 1, :], k1, k2, v, lam, q_start)
    o = rmsnorm(o, subln) * (1.0 - lambda_init)
    return o.reshape(b, s, ATT_DIM).astype(xn.dtype) @ w_o


def trunk(x, conv_state, past_k, past_v, norm_ffn1, norm_mix, norm_ffn2, ffn_w_in, ffn_w_out,
          conv_w_in, conv_w, conv_w_out, norm_kv, w_k, w_v, w_q, lambda_q1, lambda_k1,
          lambda_q2, lambda_k2, subln, w_o, norm_final):
    b, s, _ = x.shape
    q_start = past_k.shape[1]
    new_conv = []
    k1 = k2 = v_all = k_new = v_new = None
    for i in range(DEPTH):
        if i == N_A:
            kvn = rmsnorm(x, norm_kv)
            k_new = (kvn @ w_k).reshape(b, s, N_HEADS, 2 * HEAD_DIM)
            v_new = (kvn @ w_v).reshape(b, s, N_HEADS, V_DIM)
            k_all = jnp.concatenate([past_k.astype(k_new.dtype), k_new], axis=1)
            v_all = jnp.concatenate([past_v.astype(v_new.dtype), v_new], axis=1)
            k_all = k_all.reshape(b, k_all.shape[1], N_HEADS, 2, HEAD_DIM)
            k1, k2 = k_all[..., 0, :], k_all[..., 1, :]
        h = x + 0.5 * swiglu(rmsnorm(x, norm_ffn1[i]), ffn_w_in[i, 0], ffn_w_out[i, 0])
        hn = rmsnorm(h, norm_mix[i])
        if i < N_A:
            y, st = short_conv_mixer(hn, conv_w_in[i], conv_w[i], conv_w_out[i], conv_state[:, i])
            new_conv.append(st)
        else:
            j = i - N_A
            lambda_init = 0.8 - 0.6 * math.exp(-0.3 * i)
            y = diff_attn_mixer(hn, k1, k2, v_all, w_q[j], lambda_q1[j], lambda_k1[j],
                                lambda_q2[j], lambda_k2[j], subln[j], w_o[j], lambda_init, q_start)
        h = h + y
        x = h + 0.5 * swiglu(rmsnorm(h, norm_ffn2[i]), ffn_w_in[i, 1], ffn_w_out[i, 1])
    return rmsnorm(x, norm_final), jnp.stack(new_conv, axis=1), k_new, v_new


def setup_inputs(seed: int = 0) -> dict:
    key = jax.random.key(seed)
    ks = jax.random.split(key, 24)
    f32 = jnp.float32
    nrm = lambda k, shp, sc=1.0: jax.random.normal(k, shp, f32) * sc
    gain = lambda k, shp: 1.0 + 0.02 * jax.random.normal(k, shp, f32)
    return {
        "x_prompt": nrm(ks[0], (BATCH, SEQ, D_MODEL)),
        "x_sample": nrm(ks[1], (DEC_BATCH, DEC_SEQ, D_MODEL)),
        "state_conv": nrm(ks[2], (DEC_BATCH, N_A, CONV_W - 1, D_MODEL)),
        "cache_k": nrm(ks[3], (DEC_BATCH, PAST_LEN, N_HEADS, 2 * HEAD_DIM)),
        "cache_v": nrm(ks[4], (DEC_BATCH, PAST_LEN, N_HEADS, V_DIM)),
        "norm_ffn1": gain(ks[5], (DEPTH, D_MODEL)),
        "norm_mix": gain(ks[6], (DEPTH, D_MODEL)),
        "norm_ffn2": gain(ks[7], (DEPTH, D_MODEL)),
        "ffn_w_in": nrm(ks[8], (DEPTH, 2, D_MODEL, 2 * D_FF), D_MODEL ** -0.5),
        "ffn_w_out": nrm(ks[9], (DEPTH, 2, D_FF, D_MODEL), D_FF ** -0.5),
        "conv_w_in": nrm(ks[10], (N_A, D_MODEL, 3 * D_MODEL), D_MODEL ** -0.5),
        "conv_w": nrm(ks[11], (N_A, CONV_W, D_MODEL), CONV_W ** -0.5),
        "conv_w_out": nrm(ks[12], (N_A, D_MODEL, D_MODEL), D_MODEL ** -0.5),
        "norm_kv": gain(ks[13], (D_MODEL,)),
        "w_k": nrm(ks[14], (D_MODEL, N_HEADS * 2 * HEAD_DIM), D_MODEL ** -0.5),
        "w_v": nrm(ks[15], (D_MODEL, N_HEADS * V_DIM), D_MODEL ** -0.5),
        "w_q": nrm(ks[16], (N_B, D_MODEL, N_HEADS * 2 * HEAD_DIM), D_MODEL ** -0.5),
        "lambda_q1": nrm(ks[17], (N_B, HEAD_DIM), 0.1),
        "lambda_k1": nrm(ks[18], (N_B, HEAD_DIM), 0.1),
        "lambda_q2": nrm(ks[19], (N_B, HEAD_DIM), 0.1),
        "lambda_k2": nrm(ks[20], (N_B, HEAD_DIM), 0.1),
        "subln": gain(ks[21], (N_B, V_DIM)),
        "w_o": nrm(ks[22], (N_B, ATT_DIM, D_MODEL), ATT_DIM ** -0.5),
        "norm_final": gain(ks[23], (D_MODEL,)),
    }


def reference(x_prompt, x_sample, state_conv, cache_k, cache_v, norm_ffn1, norm_mix, norm_ffn2,
              ffn_w_in, ffn_w_out, conv_w_in, conv_w, conv_w_out, norm_kv, w_k, w_v, w_q,
              lambda_q1, lambda_k1, lambda_q2, lambda_k2, subln, w_o, norm_final):
    weights = (norm_ffn1, norm_mix, norm_ffn2, ffn_w_in, ffn_w_out, conv_w_in, conv_w, conv_w_out,
               norm_kv, w_k, w_v, w_q, lambda_q1, lambda_k1, lambda_q2, lambda_k2, subln, w_o,
               norm_final)
    b = x_prompt.shape[0]
    conv0 = jnp.zeros((b, N_A, CONV_W - 1, D_MODEL), x_prompt.dtype)
    empty_k = jnp.zeros((b, 0, N_HEADS, 2 * HEAD_DIM), x_prompt.dtype)
    empty_v = jnp.zeros((b, 0, N_HEADS, V_DIM), x_prompt.dtype)
    y_prompt, conv_prompt, k_prompt, v_prompt = trunk(x_prompt, conv0, empty_k, empty_v, *weights)
    y_sample, conv_sample, k_sample, v_sample = trunk(x_sample, state_conv, cache_k, cache_v, *weights)
    return (y_prompt, y_sample, conv_prompt, k_prompt, v_prompt, conv_sample, k_sample, v_sample)
```

```python
import functools
import math

import jax
import jax.numpy as jnp
from jax import lax
from jax.experimental import pallas as pl
from jax.experimental.pallas import tpu as pltpu

D_MODEL = 1024
CHUNK = 64
CONV_W = 3
D_FF = 2816
N_HEADS = 8
HEAD_DIM = 64
V_DIM = 2 * HEAD_DIM
EPS = 1e-6
NEG = -1e30
LAMBDA_INIT = 0.8 - 0.6 * math.exp(-0.3 * 1)
LOG2E = math.log2(math.e)
Q_SCALE = HEAD_DIM ** -0.5 * LOG2E

V7X_VMEM_BYTES = 64 * 1024 * 1024
VMEM_LIMIT = V7X_VMEM_BYTES - 8 * 1024 * 1024
LANES = 128

BF16 = jnp.bfloat16
F32 = jnp.float32


def _resident(shape):
    nd = len(shape)
    return pl.BlockSpec(shape, lambda *_: (0,) * nd, pipeline_mode=pl.Buffered(1))


def _rms(x, g):
    return x * lax.rsqrt(jnp.mean(x * x, axis=-1, keepdims=True) + EPS) * g


def _dot(a, b):
    return jnp.dot(a, b, preferred_element_type=F32)


def _dot_nt(a, b):
    return lax.dot_general(a, b, (((1,), (1,)), ((), ())), preferred_element_type=F32)


def _ffn_kernel(*refs, pre_proj, final_norm, proj):
    refs = list(refs)
    x_ref = refs.pop(0)
    if pre_proj:
        a_ref, wpre_ref = refs.pop(0), refs.pop(0)
    g_ref, win_ref, wout_ref = refs.pop(0), refs.pop(0), refs.pop(0)
    if final_norm:
        gf_ref = refs.pop(0)
    if proj is not None:
        gp_ref, wp_ref = refs.pop(0), refs.pop(0)
    y_ref = refs.pop(0)

    x = x_ref[...]
    if pre_proj:
        x = x + _dot(a_ref[...], wpre_ref[...])
    xn = _rms(x, g_ref[...]).astype(BF16)
    gu = _dot(xn, win_ref[...])
    g, u = gu[:, :D_FF], gu[:, D_FF:]
    act = (g * (1.0 / (1.0 + jnp.exp(-g))) * u).astype(BF16)
    y = x + 0.5 * _dot(act, wout_ref[...])
    y_ref[...] = _rms(y, gf_ref[...]) if final_norm else y

    if proj == "kv":
        kf_ref, vf_ref, kb_ref, vb_ref = refs
        r = _dot(_rms(y, gp_ref[...]).astype(BF16), wp_ref[...])
        k, v = r[:, :D_MODEL], r[:, D_MODEL:]
        kf_ref[...] = k
        vf_ref[...] = v
        kb_ref[...] = k.astype(BF16)
        vb_ref[...] = v.astype(BF16)
    elif proj == "q":
        (qb_ref,) = refs
        r = _dot(_rms(y, gp_ref[...]).astype(BF16), wp_ref[...])
        qb_ref[...] = (r * Q_SCALE).astype(BF16)


def _ffn(x, gain, w_in, w_out, *, tm, pre=None, final_gain=None, proj=None, proj_gain=None, proj_w=None):
    n = x.shape[0]
    tm = min(tm, n)
    row = lambda cols: pl.BlockSpec((tm, cols), lambda i: (i, 0))
    args, specs = [x], [row(D_MODEL)]
    if pre is not None:
        a, w_pre = pre
        args += [a, w_pre]
        specs += [row(D_MODEL), _resident(w_pre.shape)]
    args += [gain, w_in, w_out]
    specs += [_resident(gain.shape), _resident(w_in.shape), _resident(w_out.shape)]
    if final_gain is not None:
        args.append(final_gain)
        specs.append(_resident(final_gain.shape))
    out_shape = [jax.ShapeDtypeStruct((n, D_MODEL), F32)]
    out_specs = [row(D_MODEL)]
    if proj is not None:
        args += [proj_gain, proj_w]
        specs += [_resident(proj_gain.shape), _resident(proj_w.shape)]
        dts = (F32, F32, BF16, BF16) if proj == "kv" else (BF16,)
        out_shape += [jax.ShapeDtypeStruct((n, D_MODEL), dt) for dt in dts]
        out_specs += [row(D_MODEL) for _ in dts]
    return pl.pallas_call(
        functools.partial(_ffn_kernel, pre_proj=pre is not None, final_norm=final_gain is not None, proj=proj),
        grid=(n // tm,),
        in_specs=specs,
        out_specs=out_specs,
        out_shape=out_shape,
        compiler_params=pltpu.CompilerParams(dimension_semantics=("arbitrary",), vmem_limit_bytes=VMEM_LIMIT),
        name="ffn" + ("_pre" if pre is not None else "") + ("_" + proj if proj else "") + ("_fin" if final_gain is not None else ""),
    )(*args)


def _conv_kernel(x_ref, st_ref, g_ref, win_ref, cw_ref, wout_ref, o_ref, ns_ref, carry):
    tm = x_ref.shape[1]

    @pl.when(pl.program_id(1) == 0)
    def _():
        carry[0:CONV_W - 1, :] = st_ref[0, 0]

    h = x_ref[0]
    hn = _rms(h, g_ref[...]).astype(BF16)
    proj = _dot(hn, win_ref[...])
    b_g, c_g, hh = proj[:, :D_MODEL], proj[:, D_MODEL:2 * D_MODEL], proj[:, 2 * D_MODEL:]
    u = c_g * hh
    prev0, prev1 = carry[0:1, :], carry[1:2, :]
    row = lax.broadcasted_iota(jnp.int32, (tm, D_MODEL), 0)
    r1 = pltpu.roll(u, 1, 0)
    r2 = pltpu.roll(u, 2, 0)
    u1 = jnp.where(row == 0, prev1, r1)
    u2 = jnp.where(row == 0, prev0, jnp.where(row == 1, prev1, r2))
    cw = cw_ref[...]
    conv = cw[0:1, :] * u2 + cw[1:2, :] * u1 + cw[2:3, :] * u
    o_ref[0] = h + _dot((b_g * conv).astype(BF16), wout_ref[...])
    tail = r2[0:CONV_W - 1, :]
    carry[0:CONV_W - 1, :] = tail
    ns_ref[0, 0] = tail


def _conv(x, state, gain, w_in, conv_w, w_out, *, tm):
    b, s, _ = x.shape
    tm = min(tm, s)
    return pl.pallas_call(
        _conv_kernel,
        grid=(b, s // tm),
        in_specs=[
            pl.BlockSpec((1, tm, D_MODEL), lambda bi, i: (bi, i, 0)),
            pl.BlockSpec((1, 1, CONV_W - 1, D_MODEL), lambda bi, i: (bi, 0, 0, 0)),
            _resident(gain.shape), _resident(w_in.shape), _resident(conv_w.shape), _resident(w_out.shape),
        ],
        out_specs=[
            pl.BlockSpec((1, tm, D_MODEL), lambda bi, i: (bi, i, 0)),
            pl.BlockSpec((1, 1, CONV_W - 1, D_MODEL), lambda bi, i: (bi, 0, 0, 0)),
        ],
        out_shape=[jax.ShapeDtypeStruct((b, s, D_MODEL), F32),
                   jax.ShapeDtypeStruct((b, 1, CONV_W - 1, D_MODEL), F32)],
        scratch_shapes=[pltpu.VMEM((8, D_MODEL), F32)],
        compiler_params=pltpu.CompilerParams(dimension_semantics=("arbitrary", "arbitrary"),
                                             vmem_limit_bytes=VMEM_LIMIT),
        name="conv",
    )(x, state, gain, w_in, conv_w, w_out)


def _stack_q(q):
    lane = lax.broadcasted_iota(jnp.int32, q.shape, 1)
    zero = jnp.zeros_like(q)
    return jnp.concatenate([jnp.where(lane < HEAD_DIM, q, zero), jnp.where(lane >= HEAD_DIM, q, zero)], axis=0)


def _chunk_of(pos):
    return lax.shift_right_logical(pos, CHUNK.bit_length() - 1)


def _lambda(lq1, lk1, lq2, lk2):
    return (jnp.exp(jnp.sum(lq1 * lk1, axis=-1, keepdims=True))
            - jnp.exp(jnp.sum(lq2 * lk2, axis=-1, keepdims=True)) + LAMBDA_INIT)


def _finish(acc, l, t, lam, subln):
    o = acc[:t] / l[:t] - lam * (acc[t:] / l[t:])
    return (_rms(o, subln) * (1.0 - LAMBDA_INIT)).astype(BF16)


def _attn_prompt_kernel(q_ref, k_ref, v_ref, sub_ref, lq1_ref, lk1_ref, lq2_ref, lk2_ref, o_ref,
                        m_sc, l_sc, acc_sc):
    t = q_ref.shape[1]
    i = pl.program_id(2)
    qs = _stack_q(q_ref[0])
    m_sc[...] = jnp.full(m_sc.shape, NEG, F32)
    l_sc[...] = jnp.zeros(l_sc.shape, F32)
    acc_sc[...] = jnp.zeros(acc_sc.shape, F32)

    def step(j, masked):
        start = pl.multiple_of(j * t, t)
        kj = k_ref[0, pl.ds(start, t), :]
        vj = v_ref[0, pl.ds(start, t), :]
        s = _dot_nt(qs, kj)
        if masked:
            row = lax.broadcasted_iota(jnp.int32, s.shape, 0)
            col = lax.broadcasted_iota(jnp.int32, s.shape, 1)
            s = jnp.where(_chunk_of(col) <= _chunk_of(row & (t - 1)), s, NEG)
        cols = [s[:, c * LANES:(c + 1) * LANES] for c in range(t // LANES)]
        m_prev = m_sc[...]
        m_next = jnp.maximum(m_prev, jnp.max(functools.reduce(jnp.maximum, cols), axis=1, keepdims=True))
        alpha = jnp.exp2(m_prev - m_next)
        ps = [jnp.exp2(c - m_next) for c in cols]
        l_sc[...] = alpha * l_sc[...] + functools.reduce(jnp.add, ps)
        p = jnp.concatenate([c.astype(BF16) for c in ps], axis=1)
        acc_sc[...] = alpha * acc_sc[...] + _dot(p, vj)
        m_sc[...] = m_next

    def body(j, c):
        step(j, False)
        return c

    lax.fori_loop(0, i, body, 0)
    step(i, True)

    lam = _lambda(lq1_ref[...], lk1_ref[...], lq2_ref[...], lk2_ref[...])
    l = jnp.sum(l_sc[...], axis=1, keepdims=True)
    o_ref[0] = _finish(acc_sc[...], l, t, lam, sub_ref[...])


def _attn_prompt(q, k, v, subln, lq1, lk1, lq2, lk2, *, t):
    b, s, _ = q.shape
    small = lambda a: _resident(a.shape)
    return pl.pallas_call(
        _attn_prompt_kernel,
        grid=(b, N_HEADS, s // t),
        in_specs=[
            pl.BlockSpec((1, t, V_DIM), lambda bi, h, i: (bi, i, h)),
            pl.BlockSpec((1, s, V_DIM), lambda bi, h, i: (bi, 0, h)),
            pl.BlockSpec((1, s, V_DIM), lambda bi, h, i: (bi, 0, h)),
            small(subln), small(lq1), small(lk1), small(lq2), small(lk2),
        ],
        out_specs=pl.BlockSpec((1, t, V_DIM), lambda bi, h, i: (bi, i, h)),
        out_shape=jax.ShapeDtypeStruct((b, s, D_MODEL), BF16),
        scratch_shapes=[pltpu.VMEM((2 * t, LANES), F32), pltpu.VMEM((2 * t, LANES), F32),
                        pltpu.VMEM((2 * t, V_DIM), F32)],
        compiler_params=pltpu.CompilerParams(dimension_semantics=("arbitrary",) * 3, vmem_limit_bytes=VMEM_LIMIT),
        name="attn_prompt",
    )(q, k, v, subln, lq1, lk1, lq2, lk2)


def _attn_sample_kernel(q_ref, kc_ref, vc_ref, kn_ref, vn_ref, sub_ref, lq1_ref, lk1_ref, lq2_ref, lk2_ref,
                        o_ref, *, q_start):
    t = q_ref.shape[1]
    qs = _stack_q(q_ref[0])
    s_c = _dot_nt(qs, kc_ref[0].astype(BF16))
    s_n = _dot_nt(qs, kn_ref[0])
    row = lax.broadcasted_iota(jnp.int32, s_n.shape, 0)
    col = lax.broadcasted_iota(jnp.int32, s_n.shape, 1)
    s_n = jnp.where(_chunk_of(q_start + col) <= _chunk_of(q_start + (row & (t - 1))), s_n, NEG)
    m = jnp.maximum(jnp.max(s_c, axis=1, keepdims=True), jnp.max(s_n, axis=1, keepdims=True))
    p_c = jnp.exp2(s_c - m)
    p_n = jnp.exp2(s_n - m)
    l = jnp.sum(p_c, axis=1, keepdims=True) + jnp.sum(p_n, axis=1, keepdims=True)
    acc = _dot(p_c.astype(BF16), vc_ref[0].astype(BF16)) + _dot(p_n.astype(BF16), vn_ref[0])
    lam = _lambda(lq1_ref[...], lk1_ref[...], lq2_ref[...], lk2_ref[...])
    o_ref[0] = _finish(acc, l, t, lam, sub_ref[...])


def _attn_sample(q, kc, vc, kn, vn, subln, lq1, lk1, lq2, lk2):
    b, t, _ = q.shape
    past = kc.shape[1]
    small = lambda a: _resident(a.shape)
    head = lambda rows: pl.BlockSpec((1, rows, V_DIM), lambda bi, h: (bi, 0, h))
    return pl.pallas_call(
        functools.partial(_attn_sample_kernel, q_start=past),
        grid=(b, N_HEADS),
        in_specs=[head(t), head(past), head(past), head(t), head(t),
                  small(subln), small(lq1), small(lk1), small(lq2), small(lk2)],
        out_specs=head(t),
        out_shape=jax.ShapeDtypeStruct((b, t, D_MODEL), BF16),
        compiler_params=pltpu.CompilerParams(dimension_semantics=("arbitrary",) * 2, vmem_limit_bytes=VMEM_LIMIT),
        name="attn_sample",
    )(q, kc, vc, kn, vn, subln, lq1, lk1, lq2, lk2)


def _trunk(x, conv_state, cache, w, *, tm_ffn, tm_conv, t_attn):
    b, s, _ = x.shape
    n = b * s
    x2 = x.reshape(n, D_MODEL)
    (h,) = _ffn(x2, w["g_ffn1"][0], w["ffn_in"][0][0], w["ffn_out"][0][0], tm=tm_ffn)
    h, new_conv = _conv(h.reshape(b, s, D_MODEL), conv_state, w["g_mix"][0], w["conv_in"], w["conv_w"],
                        w["conv_out"], tm=tm_conv)
    x1, k_new, v_new, k_bf, v_bf = _ffn(h.reshape(n, D_MODEL), w["g_ffn2"][0], w["ffn_in"][0][1],
                                        w["ffn_out"][0][1], tm=tm_ffn, proj="kv", proj_gain=w["g_kv"],
                                        proj_w=w["w_kv"])
    h1, q_bf = _ffn(x1, w["g_ffn1"][1], w["ffn_in"][1][0], w["ffn_out"][1][0], tm=tm_ffn, proj="q",
                    proj_gain=w["g_mix"][1], proj_w=w["w_q"])
    shp = (b, s, D_MODEL)
    lam = (w["lq1"], w["lk1"], w["lq2"], w["lk2"])
    if cache is None:
        o = _attn_prompt(q_bf.reshape(shp), k_bf.reshape(shp), v_bf.reshape(shp), w["subln"], *lam, t=t_attn)
    else:
        o = _attn_sample(q_bf.reshape(shp), cache[0], cache[1], k_bf.reshape(shp), v_bf.reshape(shp),
                         w["subln"], *lam)
    (y,) = _ffn(h1, w["g_ffn2"][1], w["ffn_in"][1][1], w["ffn_out"][1][1], tm=tm_ffn,
                pre=(o.reshape(n, D_MODEL), w["w_o"]), final_gain=w["g_final"])
    kv_shape = (b, s, N_HEADS, V_DIM)
    return y.reshape(shp), new_conv, k_new.reshape(kv_shape), v_new.reshape(kv_shape)


def kernel(x_prompt, x_sample, state_conv, cache_k, cache_v, norm_ffn1, norm_mix, norm_ffn2, ffn_w_in, ffn_w_out,
           conv_w_in, conv_w, conv_w_out, norm_kv, w_k, w_v, w_q, lambda_q1, lambda_k1, lambda_q2, lambda_k2,
           subln, w_o, norm_final):
    depth = norm_ffn1.shape[0]
    row = lambda a: a.reshape(1, -1)
    w = {
        "g_ffn1": [row(norm_ffn1[i]) for i in range(depth)],
        "g_mix": [row(norm_mix[i]) for i in range(depth)],
        "g_ffn2": [row(norm_ffn2[i]) for i in range(depth)],
        "ffn_in": [[ffn_w_in[i, j].astype(BF16) for j in range(2)] for i in range(depth)],
        "ffn_out": [[ffn_w_out[i, j].astype(BF16) for j in range(2)] for i in range(depth)],
        "conv_in": conv_w_in[0].astype(BF16),
        "conv_w": conv_w[0],
        "conv_out": conv_w_out[0].astype(BF16),
        "g_kv": row(norm_kv),
        "w_kv": jnp.concatenate([w_k, w_v], axis=1).astype(BF16),
        "w_q": w_q[0].astype(BF16),
        "lq1": row(lambda_q1[0]), "lk1": row(lambda_k1[0]), "lq2": row(lambda_q2[0]), "lk2": row(lambda_k2[0]),
        "subln": row(subln[0]),
        "w_o": w_o[0].astype(BF16),
        "g_final": row(norm_final),
    }
    b = x_prompt.shape[0]
    conv0 = jnp.zeros((b, 1, CONV_W - 1, D_MODEL), x_prompt.dtype)
    y_p, conv_p, k_p, v_p = _trunk(x_prompt, conv0, None, w, tm_ffn=512, tm_conv=512, t_attn=512)
    db, past = cache_k.shape[0], cache_k.shape[1]
    cache = (cache_k.reshape(db, past, D_MODEL), cache_v.reshape(db, past, D_MODEL))
    y_s, conv_s, k_s, v_s = _trunk(x_sample, state_conv, cache, w, tm_ffn=512, tm_conv=512, t_attn=None)
    return (y_p, y_s, conv_p, k_p, v_p, conv_s, k_s, v_s)
```

```python
import functools
import math

import jax
import jax.numpy as jnp
from jax import lax
from jax.experimental import pallas as pl
from jax.experimental.pallas import tpu as pltpu

D_MODEL = 1024
CHUNK = 64
CONV_W = 3
D_FF = 2816
N_HEADS = 8
HEAD_DIM = 64
V_DIM = 2 * HEAD_DIM
EPS = 1e-6
NEG = -1e30
LAMBDA_INIT = 0.8 - 0.6 * math.exp(-0.3 * 1)
LOG2E = math.log2(math.e)
Q_SCALE = HEAD_DIM ** -0.5 * LOG2E

V7X_VMEM_BYTES = 64 * 1024 * 1024
VMEM_LIMIT = V7X_VMEM_BYTES - 8 * 1024 * 1024
LANES = 128

BF16 = jnp.bfloat16
F32 = jnp.float32


def _resident(shape):
    nd = len(shape)
    return pl.BlockSpec(shape, lambda *_: (0,) * nd, pipeline_mode=pl.Buffered(1))


def _rms(x, g):
    return x * lax.rsqrt(jnp.mean(x * x, axis=-1, keepdims=True) + EPS) * g


def _dot(a, b):
    return jnp.dot(a, b, preferred_element_type=F32)


def _dot_nt(a, b):
    return lax.dot_general(a, b, (((1,), (1,)), ((), ())), preferred_element_type=F32)


def _ffn_kernel(*refs, pre_proj, final_norm, proj):
    refs = list(refs)
    x_ref = refs.pop(0)
    if pre_proj:
        a_ref, wpre_ref = refs.pop(0), refs.pop(0)
    g_ref, win_ref, wout_ref = refs.pop(0), refs.pop(0), refs.pop(0)
    if final_norm:
        gf_ref = refs.pop(0)
    if proj is not None:
        gp_ref, wp_ref = refs.pop(0), refs.pop(0)
    if proj == "kvt":
        wvt_ref = refs.pop(0)
    y_ref = refs.pop(0)

    x = x_ref[...]
    if pre_proj:
        x = x + _dot(a_ref[...], wpre_ref[...])
    xn = _rms(x, g_ref[...]).astype(BF16)
    gu = _dot(xn, win_ref[...])
    g, u = gu[:, :D_FF], gu[:, D_FF:]
    act = (g * (1.0 / (1.0 + jnp.exp(-g))) * u).astype(BF16)
    y = x + 0.5 * _dot(act, wout_ref[...])
    y_ref[...] = _rms(y, gf_ref[...]) if final_norm else y

    if proj in ("kv", "kvt"):
        kvn = _rms(y, gp_ref[...]).astype(BF16)
        r = _dot(kvn, wp_ref[...])
        k, v = r[:, :D_MODEL], r[:, D_MODEL:]
        if proj == "kv":
            kf_ref, vf_ref, kb_ref, vb_ref = refs
            vb_ref[...] = v.astype(BF16)
        else:
            kf_ref, vf_ref, kb_ref, vtb_ref = refs
            vtb_ref[0] = _dot_nt(wvt_ref[...], kvn).astype(BF16)
        kf_ref[...] = k
        vf_ref[...] = v
        kb_ref[...] = k.astype(BF16)
    elif proj == "q":
        (qb_ref,) = refs
        r = _dot(_rms(y, gp_ref[...]).astype(BF16), wp_ref[...])
        qb_ref[...] = (r * Q_SCALE).astype(BF16)


def _ffn(x, gain, w_in, w_out, *, tm, pre=None, final_gain=None, proj=None, proj_gain=None, proj_w=None,
         w_vt=None, seq=None):
    n = x.shape[0]
    tm = min(tm, n)
    row = lambda cols: pl.BlockSpec((tm, cols), lambda i: (i, 0))
    args, specs = [x], [row(D_MODEL)]
    if pre is not None:
        a, w_pre = pre
        args += [a, w_pre]
        specs += [row(D_MODEL), _resident(w_pre.shape)]
    args += [gain, w_in, w_out]
    specs += [_resident(gain.shape), _resident(w_in.shape), _resident(w_out.shape)]
    if final_gain is not None:
        args.append(final_gain)
        specs.append(_resident(final_gain.shape))
    out_shape = [jax.ShapeDtypeStruct((n, D_MODEL), F32)]
    out_specs = [row(D_MODEL)]
    if proj is not None:
        args += [proj_gain, proj_w]
        specs += [_resident(proj_gain.shape), _resident(proj_w.shape)]
        dts = {"kv": (F32, F32, BF16, BF16), "kvt": (F32, F32, BF16), "q": (BF16,)}[proj]
        out_shape += [jax.ShapeDtypeStruct((n, D_MODEL), dt) for dt in dts]
        out_specs += [row(D_MODEL) for _ in dts]
        if proj == "kvt":
            args.append(w_vt)
            specs.append(_resident(w_vt.shape))
            nsb = seq // tm
            out_shape.append(jax.ShapeDtypeStruct((n // seq, D_MODEL, seq), BF16))
            out_specs.append(pl.BlockSpec((1, D_MODEL, tm), lambda i: (i // nsb, 0, i % nsb)))
    return pl.pallas_call(
        functools.partial(_ffn_kernel, pre_proj=pre is not None, final_norm=final_gain is not None, proj=proj),
        grid=(n // tm,),
        in_specs=specs,
        out_specs=out_specs,
        out_shape=out_shape,
        compiler_params=pltpu.CompilerParams(dimension_semantics=("arbitrary",), vmem_limit_bytes=VMEM_LIMIT),
        name="ffn" + ("_pre" if pre is not None else "") + ("_" + proj if proj else "") + ("_fin" if final_gain is not None else ""),
    )(*args)


def _conv_kernel(x_ref, st_ref, g_ref, win_ref, cw_ref, wout_ref, o_ref, ns_ref, carry):
    tm = x_ref.shape[1]

    @pl.when(pl.program_id(1) == 0)
    def _():
        carry[0:CONV_W - 1, :] = st_ref[0, 0]

    h = x_ref[0]
    hn = _rms(h, g_ref[...]).astype(BF16)
    proj = _dot(hn, win_ref[...])
    b_g, c_g, hh = proj[:, :D_MODEL], proj[:, D_MODEL:2 * D_MODEL], proj[:, 2 * D_MODEL:]
    u = c_g * hh
    prev0, prev1 = carry[0:1, :], carry[1:2, :]
    row = lax.broadcasted_iota(jnp.int32, (tm, D_MODEL), 0)
    r1 = pltpu.roll(u, 1, 0)
    r2 = pltpu.roll(u, 2, 0)
    u1 = jnp.where(row == 0, prev1, r1)
    u2 = jnp.where(row == 0, prev0, jnp.where(row == 1, prev1, r2))
    cw = cw_ref[...]
    conv = cw[0:1, :] * u2 + cw[1:2, :] * u1 + cw[2:3, :] * u
    o_ref[0] = h + _dot((b_g * conv).astype(BF16), wout_ref[...])
    tail = r2[0:CONV_W - 1, :]
    carry[0:CONV_W - 1, :] = tail
    ns_ref[0, 0] = tail


def _conv(x, state, gain, w_in, conv_w, w_out, *, tm):
    b, s, _ = x.shape
    tm = min(tm, s)
    return pl.pallas_call(
        _conv_kernel,
        grid=(b, s // tm),
        in_specs=[
            pl.BlockSpec((1, tm, D_MODEL), lambda bi, i: (bi, i, 0)),
            pl.BlockSpec((1, 1, CONV_W - 1, D_MODEL), lambda bi, i: (bi, 0, 0, 0)),
            _resident(gain.shape), _resident(w_in.shape), _resident(conv_w.shape), _resident(w_out.shape),
        ],
        out_specs=[
            pl.BlockSpec((1, tm, D_MODEL), lambda bi, i: (bi, i, 0)),
            pl.BlockSpec((1, 1, CONV_W - 1, D_MODEL), lambda bi, i: (bi, 0, 0, 0)),
        ],
        out_shape=[jax.ShapeDtypeStruct((b, s, D_MODEL), F32),
                   jax.ShapeDtypeStruct((b, 1, CONV_W - 1, D_MODEL), F32)],
        scratch_shapes=[pltpu.VMEM((8, D_MODEL), F32)],
        compiler_params=pltpu.CompilerParams(dimension_semantics=("arbitrary", "arbitrary"),
                                             vmem_limit_bytes=VMEM_LIMIT),
        name="conv",
    )(x, state, gain, w_in, conv_w, w_out)


def _stack_q(q):
    lane = lax.broadcasted_iota(jnp.int32, q.shape, 1)
    zero = jnp.zeros_like(q)
    return jnp.concatenate([jnp.where(lane < HEAD_DIM, q, zero), jnp.where(lane >= HEAD_DIM, q, zero)], axis=0)


def _chunk_of(pos):
    return lax.shift_right_logical(pos, CHUNK.bit_length() - 1)


def _lambda(lq1, lk1, lq2, lk2):
    return (jnp.exp(jnp.sum(lq1 * lk1, axis=-1, keepdims=True))
            - jnp.exp(jnp.sum(lq2 * lk2, axis=-1, keepdims=True)) + LAMBDA_INIT)


def _finish(acc, l, t, lam, subln):
    o = acc[:t] / l[:t] - lam * (acc[t:] / l[t:])
    return (_rms(o, subln) * (1.0 - LAMBDA_INIT)).astype(BF16)


def _softmax_init(m_ref, l_ref, acc_ref):
    m_ref[...] = jnp.full(m_ref.shape, NEG, F32)
    l_ref[...] = jnp.zeros(l_ref.shape, F32)
    acc_ref[...] = jnp.zeros(acc_ref.shape, F32)


def _softmax_update(s, v, m_ref, l_ref, acc_ref):
    cols = [s[:, c * LANES:(c + 1) * LANES] for c in range(s.shape[1] // LANES)]
    m_prev = m_ref[...]
    m_next = jnp.maximum(m_prev, jnp.max(functools.reduce(jnp.maximum, cols), axis=1, keepdims=True))
    alpha = jnp.exp2(m_prev - m_next)
    ps = [jnp.exp2(c - m_next) for c in cols]
    l_ref[...] = alpha * l_ref[...] + functools.reduce(jnp.add, ps)
    p = jnp.concatenate([c.astype(BF16) for c in ps], axis=1)
    acc_ref[...] = alpha * acc_ref[...] + _dot(p, v)
    m_ref[...] = m_next


ONES_ROWS = 16


def _softmax_update_t(s, vt, m_ref, acc_ref):
    m_prev = m_ref[...]
    m_next = jnp.maximum(m_prev, jnp.max(s, axis=0, keepdims=True))
    alpha = jnp.exp2(m_prev - m_next)
    p = jnp.exp2(s - m_next).astype(BF16)
    vt1 = jnp.concatenate([vt, jnp.ones((ONES_ROWS, vt.shape[1]), BF16)], axis=0)
    acc_ref[...] = alpha * acc_ref[...] + _dot(vt1, p)
    m_ref[...] = m_next


def _attn_prompt_kernel(q_ref, k_ref, vt_ref, sub_ref, lq1_ref, lk1_ref, lq2_ref, lk2_ref, o_ref,
                        s_sc, m_sc, acc_sc):
    t = q_ref.shape[1]
    i = pl.program_id(2)
    qs = _stack_q(q_ref[0])
    m_sc[...] = jnp.full(m_sc.shape, NEG, F32)
    acc_sc[...] = jnp.zeros(acc_sc.shape, F32)

    def scores(j, slot):
        kj = k_ref[0, pl.ds(pl.multiple_of(j * t, t), t), :]
        s_sc[slot] = _dot_nt(kj, qs)

    def consume(j, slot, masked):
        s = s_sc[slot]
        if masked:
            key = lax.broadcasted_iota(jnp.int32, s.shape, 0)
            qry = lax.broadcasted_iota(jnp.int32, s.shape, 1)
            s = jnp.where(_chunk_of(key) <= _chunk_of(qry & (t - 1)), s, NEG)
        vtj = vt_ref[0, :, pl.ds(pl.multiple_of(j * t, t), t)]
        _softmax_update_t(s, vtj, m_sc, acc_sc)

    scores(0, 0)

    def pair(jj, c):
        j = 2 * jj
        scores(j + 1, 1)
        consume(j, 0, False)
        scores(j + 2, 0)
        consume(j + 1, 1, False)
        return c

    lax.fori_loop(0, i // 2, pair, 0)

    @pl.when(i % 2 == 1)
    def _():
        scores(i, 1)
        consume(i - 1, 0, False)
        consume(i, 1, True)

    @pl.when(i % 2 == 0)
    def _():
        consume(i, 0, True)

    lam = _lambda(lq1_ref[...], lk1_ref[...], lq2_ref[...], lk2_ref[...])
    on = acc_sc[0:V_DIM, :] / acc_sc[V_DIM:V_DIM + 1, :]
    o = (on[:, :t] - lam * on[:, t:]).T
    o_ref[0] = (_rms(o, sub_ref[...]) * (1.0 - LAMBDA_INIT)).astype(BF16)


def _attn_prompt(q, k, vt, subln, lq1, lk1, lq2, lk2, *, t):
    b, s, _ = q.shape
    small = lambda a: _resident(a.shape)
    return pl.pallas_call(
        _attn_prompt_kernel,
        grid=(b, N_HEADS, s // t),
        in_specs=[
            pl.BlockSpec((1, t, V_DIM), lambda bi, h, i: (bi, i, h)),
            pl.BlockSpec((1, s, V_DIM), lambda bi, h, i: (bi, 0, h)),
            pl.BlockSpec((1, V_DIM, s), lambda bi, h, i: (bi, h, 0)),
            small(subln), small(lq1), small(lk1), small(lq2), small(lk2),
        ],
        out_specs=pl.BlockSpec((1, t, V_DIM), lambda bi, h, i: (bi, i, h)),
        out_shape=jax.ShapeDtypeStruct((b, s, D_MODEL), BF16),
        scratch_shapes=[pltpu.VMEM((2, t, 2 * t), F32), pltpu.VMEM((1, 2 * t), F32),
                        pltpu.VMEM((V_DIM + ONES_ROWS, 2 * t), F32)],
        compiler_params=pltpu.CompilerParams(dimension_semantics=("arbitrary",) * 3, vmem_limit_bytes=VMEM_LIMIT),
        name="attn_prompt",
    )(q, k, vt, subln, lq1, lk1, lq2, lk2)


def _attn_sample_kernel(q_ref, kc_ref, vc_ref, kn_ref, vn_ref, sub_ref, lq1_ref, lk1_ref, lq2_ref, lk2_ref,
                        o_ref, m_sc, l_sc, acc_sc, *, q_start):
    t = q_ref.shape[1]
    tkb = kc_ref.shape[1] // N_HEADS
    j = pl.program_id(1)

    @pl.when(j == 0)
    def _():
        _softmax_init(m_sc, l_sc, acc_sc)

    head_cols = lambda h: slice(h * V_DIM, (h + 1) * V_DIM)
    for h in range(N_HEADS):
        qs = _stack_q(q_ref[0, :, head_cols(h)])
        kh = kc_ref[0, pl.ds(h, tkb, stride=N_HEADS), :].astype(BF16)
        vh = vc_ref[0, pl.ds(h, tkb, stride=N_HEADS), :].astype(BF16)
        _softmax_update(_dot_nt(qs, kh), vh, m_sc.at[h], l_sc.at[h], acc_sc.at[h])

    @pl.when(j == pl.num_programs(1) - 1)
    def _():
        lam = _lambda(lq1_ref[...], lk1_ref[...], lq2_ref[...], lk2_ref[...])
        for h in range(N_HEADS):
            qs = _stack_q(q_ref[0, :, head_cols(h)])
            s_n = _dot_nt(qs, kn_ref[0, :, head_cols(h)])
            row = lax.broadcasted_iota(jnp.int32, s_n.shape, 0)
            col = lax.broadcasted_iota(jnp.int32, s_n.shape, 1)
            s_n = jnp.where(_chunk_of(q_start + col) <= _chunk_of(q_start + (row & (t - 1))), s_n, NEG)
            m_prev = m_sc[h]
            m_fin = jnp.maximum(m_prev, jnp.max(s_n, axis=1, keepdims=True))
            alpha = jnp.exp2(m_prev - m_fin)
            p_n = jnp.exp2(s_n - m_fin[:, :t])
            l = jnp.sum(alpha * l_sc[h], axis=1, keepdims=True) + jnp.sum(p_n, axis=1, keepdims=True)
            acc = alpha * acc_sc[h] + _dot(p_n.astype(BF16), vn_ref[0, :, head_cols(h)])
            o_ref[0, :, head_cols(h)] = _finish(acc, l, t, lam, sub_ref[...])


def _attn_sample(q, cache_k, cache_v, kn, vn, subln, lq1, lk1, lq2, lk2, *, tkb):
    b, t, _ = q.shape
    past = cache_k.shape[1]
    kc = cache_k.reshape(b, past * N_HEADS, V_DIM)
    vc = cache_v.reshape(b, past * N_HEADS, V_DIM)
    small = lambda a: _resident(a.shape)
    rows = pl.BlockSpec((1, t, D_MODEL), lambda bi, j: (bi, 0, 0))
    blk = pl.BlockSpec((1, tkb * N_HEADS, V_DIM), lambda bi, j: (bi, j, 0))
    return pl.pallas_call(
        functools.partial(_attn_sample_kernel, q_start=past),
        grid=(b, past // tkb),
        in_specs=[rows, blk, blk, rows, rows, small(subln), small(lq1), small(lk1), small(lq2), small(lk2)],
        out_specs=rows,
        out_shape=jax.ShapeDtypeStruct((b, t, D_MODEL), BF16),
        scratch_shapes=[pltpu.VMEM((N_HEADS, 2 * t, LANES), F32), pltpu.VMEM((N_HEADS, 2 * t, LANES), F32),
                        pltpu.VMEM((N_HEADS, 2 * t, V_DIM), F32)],
        compiler_params=pltpu.CompilerParams(dimension_semantics=("arbitrary",) * 2, vmem_limit_bytes=VMEM_LIMIT),
        name="attn_sample",
    )(q, kc, vc, kn, vn, subln, lq1, lk1, lq2, lk2)


def _trunk(x, conv_state, cache, w, *, tm_ffn, tm_conv, t_attn):
    b, s, _ = x.shape
    n = b * s
    x2 = x.reshape(n, D_MODEL)
    (h,) = _ffn(x2, w["g_ffn1"][0], w["ffn_in"][0][0], w["ffn_out"][0][0], tm=tm_ffn)
    h, new_conv = _conv(h.reshape(b, s, D_MODEL), conv_state, w["g_mix"][0], w["conv_in"], w["conv_w"],
                        w["conv_out"], tm=tm_conv)
    x1, k_new, v_new, k_bf, v_bf = _ffn(h.reshape(n, D_MODEL), w["g_ffn2"][0], w["ffn_in"][0][1],
                                        w["ffn_out"][0][1], tm=tm_ffn, proj="kvt" if cache is None else "kv",
                                        proj_gain=w["g_kv"], proj_w=w["w_kv"], w_vt=w["w_vt"], seq=s)
    h1, q_bf = _ffn(x1, w["g_ffn1"][1], w["ffn_in"][1][0], w["ffn_out"][1][0], tm=tm_ffn, proj="q",
                    proj_gain=w["g_mix"][1], proj_w=w["w_q"])
    shp = (b, s, D_MODEL)
    lam = (w["lq1"], w["lk1"], w["lq2"], w["lk2"])
    if cache is None:
        o = _attn_prompt(q_bf.reshape(shp), k_bf.reshape(shp), v_bf, w["subln"], *lam, t=t_attn)
    else:
        o = _attn_sample(q_bf.reshape(shp), cache[0], cache[1], k_bf.reshape(shp), v_bf.reshape(shp),
                         w["subln"], *lam, tkb=min(1024, cache[0].shape[1]))
    (y,) = _ffn(h1, w["g_ffn2"][1], w["ffn_in"][1][1], w["ffn_out"][1][1], tm=tm_ffn,
                pre=(o.reshape(n, D_MODEL), w["w_o"]), final_gain=w["g_final"])
    kv_shape = (b, s, N_HEADS, V_DIM)
    return y.reshape(shp), new_conv, k_new.reshape(kv_shape), v_new.reshape(kv_shape)


def kernel(x_prompt, x_sample, state_conv, cache_k, cache_v, norm_ffn1, norm_mix, norm_ffn2, ffn_w_in, ffn_w_out,
           conv_w_in, conv_w, conv_w_out, norm_kv, w_k, w_v, w_q, lambda_q1, lambda_k1, lambda_q2, lambda_k2,
           subln, w_o, norm_final):
    depth = norm_ffn1.shape[0]
    row = lambda a: a.reshape(1, -1)
    w = {
        "g_ffn1": [row(norm_ffn1[i]) for i in range(depth)],
        "g_mix": [row(norm_mix[i]) for i in range(depth)],
        "g_ffn2": [row(norm_ffn2[i]) for i in range(depth)],
        "ffn_in": [[ffn_w_in[i, j].astype(BF16) for j in range(2)] for i in range(depth)],
        "ffn_out": [[ffn_w_out[i, j].astype(BF16) for j in range(2)] for i in range(depth)],
        "conv_in": conv_w_in[0].astype(BF16),
        "conv_w": conv_w[0],
        "conv_out": conv_w_out[0].astype(BF16),
        "g_kv": row(norm_kv),
        "w_kv": jnp.concatenate([w_k, w_v], axis=1).astype(BF16),
        "w_vt": w_v.T.astype(BF16),
        "w_q": w_q[0].astype(BF16),
        "lq1": row(lambda_q1[0]), "lk1": row(lambda_k1[0]), "lq2": row(lambda_q2[0]), "lk2": row(lambda_k2[0]),
        "subln": row(subln[0]),
        "w_o": w_o[0].astype(BF16),
        "g_final": row(norm_final),
    }
    b = x_prompt.shape[0]
    conv0 = jnp.zeros((b, 1, CONV_W - 1, D_MODEL), x_prompt.dtype)
    y_p, conv_p, k_p, v_p = _trunk(x_prompt, conv0, None, w, tm_ffn=512, tm_conv=512, t_attn=512)
    y_s, conv_s, k_s, v_s = _trunk(x_sample, state_conv, (cache_k, cache_v), w, tm_ffn=512, tm_conv=512,
                                   t_attn=None)
    return (y_p, y_s, conv_p, k_p, v_p, conv_s, k_s, v_s)
```

```python
import functools
import math

import jax
import jax.numpy as jnp
from jax import lax
from jax.experimental import pallas as pl
from jax.experimental.pallas import tpu as pltpu

D_MODEL = 1024
CHUNK = 64
CONV_W = 3
D_FF = 2816
N_HEADS = 8
HEAD_DIM = 64
V_DIM = 2 * HEAD_DIM
EPS = 1e-6
NEG = -1e30
LAMBDA_INIT = 0.8 - 0.6 * math.exp(-0.3 * 1)
LOG2E = math.log2(math.e)
Q_SCALE = HEAD_DIM ** -0.5 * LOG2E

V7X_VMEM_BYTES = 64 * 1024 * 1024
VMEM_LIMIT = V7X_VMEM_BYTES - 8 * 1024 * 1024
LANES = 128

BF16 = jnp.bfloat16
F32 = jnp.float32


def _resident(shape):
    nd = len(shape)
    return pl.BlockSpec(shape, lambda *_: (0,) * nd, pipeline_mode=pl.Buffered(1))


def _rms(x, g):
    return x * lax.rsqrt(jnp.mean(x * x, axis=-1, keepdims=True) + EPS) * g


def _dot(a, b):
    return jnp.dot(a, b, preferred_element_type=F32)


def _dot_nt(a, b):
    return lax.dot_general(a, b, (((1,), (1,)), ((), ())), preferred_element_type=F32)


def _ffn_kernel(*refs, pre_proj, final_norm, proj):
    refs = list(refs)
    x_ref = refs.pop(0)
    if pre_proj:
        a_ref, wpre_ref = refs.pop(0), refs.pop(0)
    g_ref, win_ref, wout_ref = refs.pop(0), refs.pop(0), refs.pop(0)
    if final_norm:
        gf_ref = refs.pop(0)
    if proj is not None:
        gp_ref, wp_ref = refs.pop(0), refs.pop(0)
    if proj == "kvt":
        wvt_ref = refs.pop(0)
    y_ref = refs.pop(0)

    x = x_ref[...]
    if pre_proj:
        x = x + _dot(a_ref[...], wpre_ref[...])
    xn = _rms(x, g_ref[...]).astype(BF16)
    gu = _dot(xn, win_ref[...])
    g, u = gu[:, :D_FF], gu[:, D_FF:]
    act = (g * (1.0 / (1.0 + jnp.exp(-g))) * u).astype(BF16)
    y = x + 0.5 * _dot(act, wout_ref[...])
    y_ref[...] = _rms(y, gf_ref[...]) if final_norm else y

    if proj in ("kv", "kvt"):
        kvn = _rms(y, gp_ref[...]).astype(BF16)
        r = _dot(kvn, wp_ref[...])
        k, v = r[:, :D_MODEL], r[:, D_MODEL:]
        if proj == "kv":
            kf_ref, vf_ref, kb_ref, vb_ref = refs
            vb_ref[...] = v.astype(BF16)
        else:
            kf_ref, vf_ref, kb_ref, vtb_ref = refs
            vtb_ref[0] = _dot_nt(wvt_ref[...], kvn).astype(BF16)
        kf_ref[...] = k
        vf_ref[...] = v
        kb_ref[...] = k.astype(BF16)
    elif proj == "q":
        (qb_ref,) = refs
        r = _dot(_rms(y, gp_ref[...]).astype(BF16), wp_ref[...])
        qb_ref[...] = (r * Q_SCALE).astype(BF16)


def _ffn(x, gain, w_in, w_out, *, tm, pre=None, final_gain=None, proj=None, proj_gain=None, proj_w=None,
         w_vt=None, seq=None):
    n = x.shape[0]
    tm = min(tm, n)
    row = lambda cols: pl.BlockSpec((tm, cols), lambda i: (i, 0))
    args, specs = [x], [row(D_MODEL)]
    if pre is not None:
        a, w_pre = pre
        args += [a, w_pre]
        specs += [row(D_MODEL), _resident(w_pre.shape)]
    args += [gain, w_in, w_out]
    specs += [_resident(gain.shape), _resident(w_in.shape), _resident(w_out.shape)]
    if final_gain is not None:
        args.append(final_gain)
        specs.append(_resident(final_gain.shape))
    out_shape = [jax.ShapeDtypeStruct((n, D_MODEL), F32)]
    out_specs = [row(D_MODEL)]
    if proj is not None:
        args += [proj_gain, proj_w]
        specs += [_resident(proj_gain.shape), _resident(proj_w.shape)]
        dts = {"kv": (F32, F32, BF16, BF16), "kvt": (F32, F32, BF16), "q": (BF16,)}[proj]
        out_shape += [jax.ShapeDtypeStruct((n, D_MODEL), dt) for dt in dts]
        out_specs += [row(D_MODEL) for _ in dts]
        if proj == "kvt":
            args.append(w_vt)
            specs.append(_resident(w_vt.shape))
            nsb = seq // tm
            out_shape.append(jax.ShapeDtypeStruct((n // seq, D_MODEL, seq), BF16))
            out_specs.append(pl.BlockSpec((1, D_MODEL, tm), lambda i: (i // nsb, 0, i % nsb)))
    return pl.pallas_call(
        functools.partial(_ffn_kernel, pre_proj=pre is not None, final_norm=final_gain is not None, proj=proj),
        grid=(n // tm,),
        in_specs=specs,
        out_specs=out_specs,
        out_shape=out_shape,
        compiler_params=pltpu.CompilerParams(dimension_semantics=("arbitrary",), vmem_limit_bytes=VMEM_LIMIT),
        name="ffn" + ("_pre" if pre is not None else "") + ("_" + proj if proj else "") + ("_fin" if final_gain is not None else ""),
    )(*args)


def _conv_kernel(x_ref, st_ref, g_ref, win_ref, cw_ref, wout_ref, o_ref, ns_ref, carry):
    tm = x_ref.shape[1]

    @pl.when(pl.program_id(1) == 0)
    def _():
        carry[0:CONV_W - 1, :] = st_ref[0, 0]

    h = x_ref[0]
    hn = _rms(h, g_ref[...]).astype(BF16)
    proj = _dot(hn, win_ref[...])
    b_g, c_g, hh = proj[:, :D_MODEL], proj[:, D_MODEL:2 * D_MODEL], proj[:, 2 * D_MODEL:]
    u = c_g * hh
    prev0, prev1 = carry[0:1, :], carry[1:2, :]
    row = lax.broadcasted_iota(jnp.int32, (tm, D_MODEL), 0)
    r1 = pltpu.roll(u, 1, 0)
    r2 = pltpu.roll(u, 2, 0)
    u1 = jnp.where(row == 0, prev1, r1)
    u2 = jnp.where(row == 0, prev0, jnp.where(row == 1, prev1, r2))
    cw = cw_ref[...]
    conv = cw[0:1, :] * u2 + cw[1:2, :] * u1 + cw[2:3, :] * u
    o_ref[0] = h + _dot((b_g * conv).astype(BF16), wout_ref[...])
    tail = r2[0:CONV_W - 1, :]
    carry[0:CONV_W - 1, :] = tail
    ns_ref[0, 0] = tail


def _conv(x, state, gain, w_in, conv_w, w_out, *, tm):
    b, s, _ = x.shape
    tm = min(tm, s)
    return pl.pallas_call(
        _conv_kernel,
        grid=(b, s // tm),
        in_specs=[
            pl.BlockSpec((1, tm, D_MODEL), lambda bi, i: (bi, i, 0)),
            pl.BlockSpec((1, 1, CONV_W - 1, D_MODEL), lambda bi, i: (bi, 0, 0, 0)),
            _resident(gain.shape), _resident(w_in.shape), _resident(conv_w.shape), _resident(w_out.shape),
        ],
        out_specs=[
            pl.BlockSpec((1, tm, D_MODEL), lambda bi, i: (bi, i, 0)),
            pl.BlockSpec((1, 1, CONV_W - 1, D_MODEL), lambda bi, i: (bi, 0, 0, 0)),
        ],
        out_shape=[jax.ShapeDtypeStruct((b, s, D_MODEL), F32),
                   jax.ShapeDtypeStruct((b, 1, CONV_W - 1, D_MODEL), F32)],
        scratch_shapes=[pltpu.VMEM((8, D_MODEL), F32)],
        compiler_params=pltpu.CompilerParams(dimension_semantics=("arbitrary", "arbitrary"),
                                             vmem_limit_bytes=VMEM_LIMIT),
        name="conv",
    )(x, state, gain, w_in, conv_w, w_out)


def _stack_q(q):
    lane = lax.broadcasted_iota(jnp.int32, q.shape, 1)
    zero = jnp.zeros_like(q)
    return jnp.concatenate([jnp.where(lane < HEAD_DIM, q, zero), jnp.where(lane >= HEAD_DIM, q, zero)], axis=0)


def _chunk_of(pos):
    return lax.shift_right_logical(pos, CHUNK.bit_length() - 1)


def _lambda(lq1, lk1, lq2, lk2):
    return (jnp.exp(jnp.sum(lq1 * lk1, axis=-1, keepdims=True))
            - jnp.exp(jnp.sum(lq2 * lk2, axis=-1, keepdims=True)) + LAMBDA_INIT)


def _finish(acc, l, t, lam, subln):
    o = acc[:t] / l[:t] - lam * (acc[t:] / l[t:])
    return (_rms(o, subln) * (1.0 - LAMBDA_INIT)).astype(BF16)


def _softmax_init(m_ref, l_ref, acc_ref):
    m_ref[...] = jnp.full(m_ref.shape, NEG, F32)
    l_ref[...] = jnp.zeros(l_ref.shape, F32)
    acc_ref[...] = jnp.zeros(acc_ref.shape, F32)


def _softmax_update(s, v, m_ref, l_ref, acc_ref):
    cols = [s[:, c * LANES:(c + 1) * LANES] for c in range(s.shape[1] // LANES)]
    m_prev = m_ref[...]
    m_next = jnp.maximum(m_prev, jnp.max(functools.reduce(jnp.maximum, cols), axis=1, keepdims=True))
    alpha = jnp.exp2(m_prev - m_next)
    ps = [jnp.exp2(c - m_next) for c in cols]
    l_ref[...] = alpha * l_ref[...] + functools.reduce(jnp.add, ps)
    p = jnp.concatenate([c.astype(BF16) for c in ps], axis=1)
    acc_ref[...] = alpha * acc_ref[...] + _dot(p, v)
    m_ref[...] = m_next


ONES_ROWS = 16


def _softmax_update_t(s, cmax, vt, m_ref, acc_ref):
    m_prev = m_ref[...]
    m_next = jnp.maximum(m_prev, cmax)
    alpha = jnp.exp2(m_prev - m_next)
    p = jnp.exp2(s - m_next).astype(BF16)
    vt1 = jnp.concatenate([vt, jnp.ones((ONES_ROWS, vt.shape[1]), BF16)], axis=0)
    acc_ref[...] = alpha * acc_ref[...] + _dot(vt1, p)
    m_ref[...] = m_next


SLOT_A, SLOT_B, SLOT_NEXT = 0, 1, 2


def _attn_prompt_kernel(q_ref, qn_ref, k_ref, vt_ref, sub_ref, lq1_ref, lk1_ref, lq2_ref, lk2_ref, o_ref,
                        s_sc, cm_sc, m_sc, acc_sc):
    t = q_ref.shape[1]
    i = pl.program_id(2)
    qs = _stack_q(q_ref[0])
    m_sc[...] = jnp.full(m_sc.shape, NEG, F32)
    acc_sc[...] = jnp.zeros(acc_sc.shape, F32)

    def store_scores(s, slot):
        s_sc[slot] = s
        cm_sc[slot] = jnp.max(s, axis=0, keepdims=True)

    def S(j, slot, masked=False):
        kj = k_ref[0, pl.ds(pl.multiple_of(j * t, t), t), :]
        s = _dot_nt(kj, qs)
        if masked:
            key = lax.broadcasted_iota(jnp.int32, s.shape, 0)
            qry = lax.broadcasted_iota(jnp.int32, s.shape, 1)
            s = jnp.where(_chunk_of(key) <= _chunk_of(qry & (t - 1)), s, NEG)
        store_scores(s, slot)

    def N():
        store_scores(_dot_nt(k_ref[0, 0:t, :], _stack_q(qn_ref[0])), SLOT_NEXT)

    def C(j, slot):
        vtj = vt_ref[0, :, pl.ds(pl.multiple_of(j * t, t), t)]
        _softmax_update_t(s_sc[slot], cm_sc[slot], vtj, m_sc, acc_sc)

    @pl.when(i == 0)
    def _():
        S(0, SLOT_A, True)
        N()
        C(0, SLOT_A)

    @pl.when(i == 1)
    def _():
        S(1, SLOT_A, True)
        C(0, SLOT_NEXT)
        N()
        C(1, SLOT_A)

    @pl.when(i >= 2)
    def _():
        S(1, SLOT_A)
        C(0, SLOT_NEXT)

        def pair(jj, c):
            j = 1 + 2 * jj
            S(j + 1, SLOT_B)
            C(j, SLOT_A)
            S(j + 2, SLOT_A)
            C(j + 1, SLOT_B)
            return c

        lax.fori_loop(0, (i - 2) // 2, pair, 0)

        @pl.when(i % 2 == 0)
        def _():
            S(i, SLOT_B, True)
            C(i - 1, SLOT_A)
            N()
            C(i, SLOT_B)

        @pl.when(i % 2 == 1)
        def _():
            S(i - 1, SLOT_B)
            C(i - 2, SLOT_A)
            S(i, SLOT_A, True)
            C(i - 1, SLOT_B)
            N()
            C(i, SLOT_A)

    lam = _lambda(lq1_ref[...], lk1_ref[...], lq2_ref[...], lk2_ref[...])
    on = acc_sc[0:V_DIM, :] / acc_sc[V_DIM:V_DIM + 1, :]
    o = (on[:, :t] - lam * on[:, t:]).T
    o_ref[0] = (_rms(o, sub_ref[...]) * (1.0 - LAMBDA_INIT)).astype(BF16)


def _attn_prompt(q, k, vt, subln, lq1, lk1, lq2, lk2, *, t):
    b, s, _ = q.shape
    nq = s // t
    small = lambda a: _resident(a.shape)
    return pl.pallas_call(
        _attn_prompt_kernel,
        grid=(b, N_HEADS, nq),
        in_specs=[
            pl.BlockSpec((1, t, V_DIM), lambda bi, h, i: (bi, i, h)),
            pl.BlockSpec((1, t, V_DIM), lambda bi, h, i: (bi, jnp.minimum(i + 1, nq - 1), h)),
            pl.BlockSpec((1, s, V_DIM), lambda bi, h, i: (bi, 0, h)),
            pl.BlockSpec((1, V_DIM, s), lambda bi, h, i: (bi, h, 0)),
            small(subln), small(lq1), small(lk1), small(lq2), small(lk2),
        ],
        out_specs=pl.BlockSpec((1, t, V_DIM), lambda bi, h, i: (bi, i, h)),
        out_shape=jax.ShapeDtypeStruct((b, s, D_MODEL), BF16),
        scratch_shapes=[pltpu.VMEM((3, t, 2 * t), F32), pltpu.VMEM((3, 1, 2 * t), F32),
                        pltpu.VMEM((1, 2 * t), F32), pltpu.VMEM((V_DIM + ONES_ROWS, 2 * t), F32)],
        compiler_params=pltpu.CompilerParams(dimension_semantics=("arbitrary",) * 3, vmem_limit_bytes=VMEM_LIMIT),
        name="attn_prompt",
    )(q, q, k, vt, subln, lq1, lk1, lq2, lk2)


def _attn_sample_kernel(q_ref, kc_ref, vc_ref, kn_ref, vn_ref, sub_ref, lq1_ref, lk1_ref, lq2_ref, lk2_ref,
                        o_ref, m_sc, l_sc, acc_sc, *, q_start):
    t = q_ref.shape[1]
    tkb = kc_ref.shape[1] // N_HEADS
    j = pl.program_id(1)

    @pl.when(j == 0)
    def _():
        _softmax_init(m_sc, l_sc, acc_sc)

    head_cols = lambda h: slice(h * V_DIM, (h + 1) * V_DIM)
    for h in range(N_HEADS):
        qs = _stack_q(q_ref[0, :, head_cols(h)])
        kh = kc_ref[0, pl.ds(h, tkb, stride=N_HEADS), :].astype(BF16)
        vh = vc_ref[0, pl.ds(h, tkb, stride=N_HEADS), :].astype(BF16)
        _softmax_update(_dot_nt(qs, kh), vh, m_sc.at[h], l_sc.at[h], acc_sc.at[h])

    @pl.when(j == pl.num_programs(1) - 1)
    def _():
        lam = _lambda(lq1_ref[...], lk1_ref[...], lq2_ref[...], lk2_ref[...])
        for h in range(N_HEADS):
            qs = _stack_q(q_ref[0, :, head_cols(h)])
            s_n = _dot_nt(qs, kn_ref[0, :, head_cols(h)])
            row = lax.broadcasted_iota(jnp.int32, s_n.shape, 0)
            col = lax.broadcasted_iota(jnp.int32, s_n.shape, 1)
            s_n = jnp.where(_chunk_of(q_start + col) <= _chunk_of(q_start + (row & (t - 1))), s_n, NEG)
            m_prev = m_sc[h]
            m_fin = jnp.maximum(m_prev, jnp.max(s_n, axis=1, keepdims=True))
            alpha = jnp.exp2(m_prev - m_fin)
            p_n = jnp.exp2(s_n - m_fin[:, :t])
            l = jnp.sum(alpha * l_sc[h], axis=1, keepdims=True) + jnp.sum(p_n, axis=1, keepdims=True)
            acc = alpha * acc_sc[h] + _dot(p_n.astype(BF16), vn_ref[0, :, head_cols(h)])
            o_ref[0, :, head_cols(h)] = _finish(acc, l, t, lam, sub_ref[...])


def _attn_sample(q, cache_k, cache_v, kn, vn, subln, lq1, lk1, lq2, lk2, *, tkb):
    b, t, _ = q.shape
    past = cache_k.shape[1]
    kc = cache_k.reshape(b, past * N_HEADS, V_DIM)
    vc = cache_v.reshape(b, past * N_HEADS, V_DIM)
    small = lambda a: _resident(a.shape)
    rows = pl.BlockSpec((1, t, D_MODEL), lambda bi, j: (bi, 0, 0))
    blk = pl.BlockSpec((1, tkb * N_HEADS, V_DIM), lambda bi, j: (bi, j, 0))
    return pl.pallas_call(
        functools.partial(_attn_sample_kernel, q_start=past),
        grid=(b, past // tkb),
        in_specs=[rows, blk, blk, rows, rows, small(subln), small(lq1), small(lk1), small(lq2), small(lk2)],
        out_specs=rows,
        out_shape=jax.ShapeDtypeStruct((b, t, D_MODEL), BF16),
        scratch_shapes=[pltpu.VMEM((N_HEADS, 2 * t, LANES), F32), pltpu.VMEM((N_HEADS, 2 * t, LANES), F32),
                        pltpu.VMEM((N_HEADS, 2 * t, V_DIM), F32)],
        compiler_params=pltpu.CompilerParams(dimension_semantics=("arbitrary",) * 2, vmem_limit_bytes=VMEM_LIMIT),
        name="attn_sample",
    )(q, kc, vc, kn, vn, subln, lq1, lk1, lq2, lk2)


def _trunk(x, conv_state, cache, w, *, tm_ffn, tm_conv, t_attn):
    b, s, _ = x.shape
    n = b * s
    x2 = x.reshape(n, D_MODEL)
    (h,) = _ffn(x2, w["g_ffn1"][0], w["ffn_in"][0][0], w["ffn_out"][0][0], tm=tm_ffn)
    h, new_conv = _conv(h.reshape(b, s, D_MODEL), conv_state, w["g_mix"][0], w["conv_in"], w["conv_w"],
                        w["conv_out"], tm=tm_conv)
    x1, k_new, v_new, k_bf, v_bf = _ffn(h.reshape(n, D_MODEL), w["g_ffn2"][0], w["ffn_in"][0][1],
                                        w["ffn_out"][0][1], tm=tm_ffn, proj="kvt" if cache is None else "kv",
                                        proj_gain=w["g_kv"], proj_w=w["w_kv"], w_vt=w["w_vt"], seq=s)
    h1, q_bf = _ffn(x1, w["g_ffn1"][1], w["ffn_in"][1][0], w["ffn_out"][1][0], tm=tm_ffn, proj="q",
                    proj_gain=w["g_mix"][1], proj_w=w["w_q"])
    shp = (b, s, D_MODEL)
    lam = (w["lq1"], w["lk1"], w["lq2"], w["lk2"])
    if cache is None:
        o = _attn_prompt(q_bf.reshape(shp), k_bf.reshape(shp), v_bf, w["subln"], *lam, t=t_attn)
    else:
        o = _attn_sample(q_bf.reshape(shp), cache[0], cache[1], k_bf.reshape(shp), v_bf.reshape(shp),
                         w["subln"], *lam, tkb=min(1024, cache[0].shape[1]))
    (y,) = _ffn(h1, w["g_ffn2"][1], w["ffn_in"][1][1], w["ffn_out"][1][1], tm=tm_ffn,
                pre=(o.reshape(n, D_MODEL), w["w_o"]), final_gain=w["g_final"])
    kv_shape = (b, s, N_HEADS, V_DIM)
    return y.reshape(shp), new_conv, k_new.reshape(kv_shape), v_new.reshape(kv_shape)


def kernel(x_prompt, x_sample, state_conv, cache_k, cache_v, norm_ffn1, norm_mix, norm_ffn2, ffn_w_in, ffn_w_out,
           conv_w_in, conv_w, conv_w_out, norm_kv, w_k, w_v, w_q, lambda_q1, lambda_k1, lambda_q2, lambda_k2,
           subln, w_o, norm_final):
    depth = norm_ffn1.shape[0]
    row = lambda a: a.reshape(1, -1)
    w = {
        "g_ffn1": [row(norm_ffn1[i]) for i in range(depth)],
        "g_mix": [row(norm_mix[i]) for i in range(depth)],
        "g_ffn2": [row(norm_ffn2[i]) for i in range(depth)],
        "ffn_in": [[ffn_w_in[i, j].astype(BF16) for j in range(2)] for i in range(depth)],
        "ffn_out": [[ffn_w_out[i, j].astype(BF16) for j in range(2)] for i in range(depth)],
        "conv_in": conv_w_in[0].astype(BF16),
        "conv_w": conv_w[0],
        "conv_out": conv_w_out[0].astype(BF16),
        "g_kv": row(norm_kv),
        "w_kv": jnp.concatenate([w_k, w_v], axis=1).astype(BF16),
        "w_vt": w_v.T.astype(BF16),
        "w_q": w_q[0].astype(BF16),
        "lq1": row(lambda_q1[0]), "lk1": row(lambda_k1[0]), "lq2": row(lambda_q2[0]), "lk2": row(lambda_k2[0]),
        "subln": row(subln[0]),
        "w_o": w_o[0].astype(BF16),
        "g_final": row(norm_final),
    }
    b = x_prompt.shape[0]
    conv0 = jnp.zeros((b, 1, CONV_W - 1, D_MODEL), x_prompt.dtype)
    y_p, conv_p, k_p, v_p = _trunk(x_prompt, conv0, None, w, tm_ffn=512, tm_conv=512, t_attn=512)
    y_s, conv_s, k_s, v_s = _trunk(x_sample, state_conv, (cache_k, cache_v), w, tm_ffn=512, tm_conv=512,
                                   t_attn=None)
    return (y_p, y_s, conv_p, k_p, v_p, conv_s, k_s, v_s)
```

```python
import functools
import math

import jax
import jax.numpy as jnp
from jax import lax
from jax.experimental import pallas as pl
from jax.experimental.pallas import tpu as pltpu

D_MODEL = 1024
CHUNK = 64
CONV_W = 3
D_FF = 2816
N_HEADS = 8
HEAD_DIM = 64
V_DIM = 2 * HEAD_DIM
EPS = 1e-6
NEG = -1e30
LAMBDA_INIT = 0.8 - 0.6 * math.exp(-0.3 * 1)
LOG2E = math.log2(math.e)
Q_SCALE = HEAD_DIM ** -0.5 * LOG2E

V7X_VMEM_BYTES = 64 * 1024 * 1024
VMEM_LIMIT = V7X_VMEM_BYTES - 8 * 1024 * 1024
LANES = 128

BF16 = jnp.bfloat16
F32 = jnp.float32


def _resident(shape):
    nd = len(shape)
    return pl.BlockSpec(shape, lambda *_: (0,) * nd, pipeline_mode=pl.Buffered(1))


def _rms(x, g):
    return x * lax.rsqrt(jnp.mean(x * x, axis=-1, keepdims=True) + EPS) * g


def _dot(a, b):
    return jnp.dot(a, b, preferred_element_type=F32)


def _dot_nt(a, b):
    return lax.dot_general(a, b, (((1,), (1,)), ((), ())), preferred_element_type=F32)


def _ffn_kernel(*refs, pre_proj, final_norm, proj):
    refs = list(refs)
    x_ref = refs.pop(0)
    if pre_proj:
        a_ref, wpre_ref = refs.pop(0), refs.pop(0)
    g_ref, win_ref, wout_ref = refs.pop(0), refs.pop(0), refs.pop(0)
    if final_norm:
        gf_ref = refs.pop(0)
    if proj is not None:
        gp_ref, wp_ref = refs.pop(0), refs.pop(0)
    if proj == "kvt":
        wvt_ref = refs.pop(0)
    y_ref = refs.pop(0)

    x = x_ref[...]
    if pre_proj == "rows":
        x = x + _dot(a_ref[...], wpre_ref[...])
    elif pre_proj == "cols":
        x = x + lax.dot_general(a_ref[0], wpre_ref[...], (((0,), (0,)), ((), ())), preferred_element_type=F32)
    xn = _rms(x, g_ref[...]).astype(BF16)
    gu = _dot(xn, win_ref[...])
    g, u = gu[:, :D_FF], gu[:, D_FF:]
    act = (g * (1.0 / (1.0 + jnp.exp(-g))) * u).astype(BF16)
    y = x + 0.5 * _dot(act, wout_ref[...])
    y_ref[...] = _rms(y, gf_ref[...]) if final_norm else y

    if proj in ("kv", "kvt"):
        kvn = _rms(y, gp_ref[...]).astype(BF16)
        r = _dot(kvn, wp_ref[...])
        k, v = r[:, :D_MODEL], r[:, D_MODEL:]
        if proj == "kv":
            kf_ref, vf_ref, kb_ref, vb_ref = refs
            vb_ref[...] = v.astype(BF16)
        else:
            kf_ref, vf_ref, kb_ref, vtb_ref = refs
            vtb_ref[0] = _dot_nt(wvt_ref[...], kvn).astype(BF16)
        kf_ref[...] = k
        vf_ref[...] = v
        kb_ref[...] = k.astype(BF16)
    elif proj == "q":
        (qb_ref,) = refs
        r = _dot(_rms(y, gp_ref[...]).astype(BF16), wp_ref[...])
        qb_ref[...] = (r * Q_SCALE).astype(BF16)


def _ffn(x, gain, w_in, w_out, *, tm, pre=None, final_gain=None, proj=None, proj_gain=None, proj_w=None,
         w_vt=None, seq=None):
    n = x.shape[0]
    tm = min(tm, n)
    row = lambda cols: pl.BlockSpec((tm, cols), lambda i: (i, 0))
    args, specs = [x], [row(D_MODEL)]
    pre_kind = None
    if pre is not None:
        a, w_pre = pre
        args += [a, w_pre]
        if a.ndim == 2:
            pre_kind = "rows"
            specs += [row(D_MODEL), _resident(w_pre.shape)]
        else:
            pre_kind = "cols"
            nsb_a = a.shape[2] // tm
            specs += [pl.BlockSpec((1, D_MODEL, tm), lambda i: (i // nsb_a, 0, i % nsb_a)), _resident(w_pre.shape)]
    args += [gain, w_in, w_out]
    specs += [_resident(gain.shape), _resident(w_in.shape), _resident(w_out.shape)]
    if final_gain is not None:
        args.append(final_gain)
        specs.append(_resident(final_gain.shape))
    out_shape = [jax.ShapeDtypeStruct((n, D_MODEL), F32)]
    out_specs = [row(D_MODEL)]
    if proj is not None:
        args += [proj_gain, proj_w]
        specs += [_resident(proj_gain.shape), _resident(proj_w.shape)]
        dts = {"kv": (F32, F32, BF16, BF16), "kvt": (F32, F32, BF16), "q": (BF16,)}[proj]
        out_shape += [jax.ShapeDtypeStruct((n, D_MODEL), dt) for dt in dts]
        out_specs += [row(D_MODEL) for _ in dts]
        if proj == "kvt":
            args.append(w_vt)
            specs.append(_resident(w_vt.shape))
            nsb = seq // tm
            out_shape.append(jax.ShapeDtypeStruct((n // seq, D_MODEL, seq), BF16))
            out_specs.append(pl.BlockSpec((1, D_MODEL, tm), lambda i: (i // nsb, 0, i % nsb)))
    return pl.pallas_call(
        functools.partial(_ffn_kernel, pre_proj=pre_kind, final_norm=final_gain is not None, proj=proj),
        grid=(n // tm,),
        in_specs=specs,
        out_specs=out_specs,
        out_shape=out_shape,
        compiler_params=pltpu.CompilerParams(dimension_semantics=("arbitrary",), vmem_limit_bytes=VMEM_LIMIT),
        name="ffn" + ("_pre" if pre is not None else "") + ("_" + proj if proj else "") + ("_fin" if final_gain is not None else ""),
    )(*args)


def _conv_kernel(x_ref, st_ref, g_ref, win_ref, cw_ref, wout_ref, o_ref, ns_ref, carry):
    tm = x_ref.shape[1]

    @pl.when(pl.program_id(1) == 0)
    def _():
        carry[0:CONV_W - 1, :] = st_ref[0, 0]

    h = x_ref[0]
    hn = _rms(h, g_ref[...]).astype(BF16)
    proj = _dot(hn, win_ref[...])
    b_g, c_g, hh = proj[:, :D_MODEL], proj[:, D_MODEL:2 * D_MODEL], proj[:, 2 * D_MODEL:]
    u = c_g * hh
    prev0, prev1 = carry[0:1, :], carry[1:2, :]
    row = lax.broadcasted_iota(jnp.int32, (tm, D_MODEL), 0)
    r1 = pltpu.roll(u, 1, 0)
    r2 = pltpu.roll(u, 2, 0)
    u1 = jnp.where(row == 0, prev1, r1)
    u2 = jnp.where(row == 0, prev0, jnp.where(row == 1, prev1, r2))
    cw = cw_ref[...]
    conv = cw[0:1, :] * u2 + cw[1:2, :] * u1 + cw[2:3, :] * u
    o_ref[0] = h + _dot((b_g * conv).astype(BF16), wout_ref[...])
    tail = r2[0:CONV_W - 1, :]
    carry[0:CONV_W - 1, :] = tail
    ns_ref[0, 0] = tail


def _conv_streams_kernel(x_ref, st_ref, g_ref, win_ref, cw_ref, wout_ref, o_ref, ns_ref, *, s):
    n = x_ref.shape[0]
    b = n // s
    h = x_ref[...]
    hn = _rms(h, g_ref[...]).astype(BF16)
    proj = _dot(hn, win_ref[...])
    b_g, c_g, hh = proj[:, :D_MODEL], proj[:, D_MODEL:2 * D_MODEL], proj[:, 2 * D_MODEL:]
    u = c_g * hh
    st = st_ref[...]
    per_row = lambda rows: jnp.broadcast_to(rows, (b, s, D_MODEL)).reshape(n, D_MODEL)
    prev0, prev1 = per_row(st[:, 0:1, :]), per_row(st[:, 1:2, :])
    pos = lax.broadcasted_iota(jnp.int32, (n, D_MODEL), 0) & (s - 1)
    u1 = jnp.where(pos == 0, prev1, pltpu.roll(u, 1, 0))
    u2 = jnp.where(pos == 0, prev0, jnp.where(pos == 1, prev1, pltpu.roll(u, 2, 0)))
    cw = cw_ref[...]
    conv = cw[0:1, :] * u2 + cw[1:2, :] * u1 + cw[2:3, :] * u
    o_ref[...] = h + _dot((b_g * conv).astype(BF16), wout_ref[...])
    ns_ref[...] = pltpu.roll(u, n - (s - (CONV_W - 1)), 0).reshape(b, s, D_MODEL)[:, 0:CONV_W - 1, :]


def _conv_streams(x, state, gain, w_in, conv_w, w_out):
    b, s, _ = x.shape
    assert s & (s - 1) == 0 and s % 8 == 0, s
    n = b * s
    full = lambda shape: pl.BlockSpec(shape, lambda i: (0,) * len(shape))
    y, ns = pl.pallas_call(
        functools.partial(_conv_streams_kernel, s=s),
        grid=(1,),
        in_specs=[full((n, D_MODEL)), full((b, CONV_W - 1, D_MODEL)),
                  full(gain.shape), full(w_in.shape), full(conv_w.shape), full(w_out.shape)],
        out_specs=[full((n, D_MODEL)), full((b, CONV_W - 1, D_MODEL))],
        out_shape=[jax.ShapeDtypeStruct((n, D_MODEL), F32), jax.ShapeDtypeStruct((b, CONV_W - 1, D_MODEL), F32)],
        compiler_params=pltpu.CompilerParams(dimension_semantics=("arbitrary",), vmem_limit_bytes=VMEM_LIMIT),
        name="conv_streams",
    )(x.reshape(n, D_MODEL), state.reshape(b, CONV_W - 1, D_MODEL), gain, w_in, conv_w, w_out)
    return y.reshape(b, s, D_MODEL), ns.reshape(b, 1, CONV_W - 1, D_MODEL)


def _conv(x, state, gain, w_in, conv_w, w_out, *, tm):
    b, s, _ = x.shape
    tm = min(tm, s)
    return pl.pallas_call(
        _conv_kernel,
        grid=(b, s // tm),
        in_specs=[
            pl.BlockSpec((1, tm, D_MODEL), lambda bi, i: (bi, i, 0)),
            pl.BlockSpec((1, 1, CONV_W - 1, D_MODEL), lambda bi, i: (bi, 0, 0, 0)),
            _resident(gain.shape), _resident(w_in.shape), _resident(conv_w.shape), _resident(w_out.shape),
        ],
        out_specs=[
            pl.BlockSpec((1, tm, D_MODEL), lambda bi, i: (bi, i, 0)),
            pl.BlockSpec((1, 1, CONV_W - 1, D_MODEL), lambda bi, i: (bi, 0, 0, 0)),
        ],
        out_shape=[jax.ShapeDtypeStruct((b, s, D_MODEL), F32),
                   jax.ShapeDtypeStruct((b, 1, CONV_W - 1, D_MODEL), F32)],
        scratch_shapes=[pltpu.VMEM((8, D_MODEL), F32)],
        compiler_params=pltpu.CompilerParams(dimension_semantics=("arbitrary", "arbitrary"),
                                             vmem_limit_bytes=VMEM_LIMIT),
        name="conv",
    )(x, state, gain, w_in, conv_w, w_out)


def _stack_q(q):
    lane = lax.broadcasted_iota(jnp.int32, q.shape, 1)
    zero = jnp.zeros_like(q)
    return jnp.concatenate([jnp.where(lane < HEAD_DIM, q, zero), jnp.where(lane >= HEAD_DIM, q, zero)], axis=0)


def _chunk_of(pos):
    return lax.shift_right_logical(pos, CHUNK.bit_length() - 1)


def _lambda(lq1, lk1, lq2, lk2):
    return (jnp.exp(jnp.sum(lq1 * lk1, axis=-1, keepdims=True))
            - jnp.exp(jnp.sum(lq2 * lk2, axis=-1, keepdims=True)) + LAMBDA_INIT)


def _finish(acc, l, t, lam, subln):
    o = acc[:t] / l[:t] - lam * (acc[t:] / l[t:])
    return (_rms(o, subln) * (1.0 - LAMBDA_INIT)).astype(BF16)


def _softmax_init(m_ref, l_ref, acc_ref):
    m_ref[...] = jnp.full(m_ref.shape, NEG, F32)
    l_ref[...] = jnp.zeros(l_ref.shape, F32)
    acc_ref[...] = jnp.zeros(acc_ref.shape, F32)


def _softmax_update(s, v, m_ref, l_ref, acc_ref):
    cols = [s[:, c * LANES:(c + 1) * LANES] for c in range(s.shape[1] // LANES)]
    m_prev = m_ref[...]
    m_next = jnp.maximum(m_prev, jnp.max(functools.reduce(jnp.maximum, cols), axis=1, keepdims=True))
    alpha = jnp.exp2(m_prev - m_next)
    ps = [jnp.exp2(c - m_next) for c in cols]
    l_ref[...] = alpha * l_ref[...] + functools.reduce(jnp.add, ps)
    p = jnp.concatenate([c.astype(BF16) for c in ps], axis=1)
    acc_ref[...] = alpha * acc_ref[...] + _dot(p, v)
    m_ref[...] = m_next


ONES_ROWS = 16


def _softmax_update_t(s, cmax, vt, m_ref, acc_ref):
    m_prev = m_ref[...]
    m_next = jnp.maximum(m_prev, cmax)
    alpha = jnp.exp2(m_prev - m_next)
    p = jnp.exp2(s - m_next).astype(BF16)
    vt1 = jnp.concatenate([vt, jnp.ones((ONES_ROWS, vt.shape[1]), BF16)], axis=0)
    acc_ref[...] = alpha * acc_ref[...] + _dot(vt1, p)
    m_ref[...] = m_next


SLOT_A, SLOT_B, SLOT_NEXT = 0, 1, 2


def _attn_prompt_kernel(q_ref, qn_ref, k_ref, vt_ref, sub_ref, lq1_ref, lk1_ref, lq2_ref, lk2_ref, o_ref,
                        s_sc, cm_sc, m_sc, acc_sc):
    t = q_ref.shape[1]
    i = pl.program_id(2)
    qs = _stack_q(q_ref[0])
    m_sc[...] = jnp.full(m_sc.shape, NEG, F32)
    acc_sc[...] = jnp.zeros(acc_sc.shape, F32)

    def store_scores(s, slot):
        s_sc[slot] = s
        cm_sc[slot] = jnp.max(s, axis=0, keepdims=True)

    def S(j, slot, masked=False):
        kj = k_ref[0, pl.ds(pl.multiple_of(j * t, t), t), :]
        s = _dot_nt(kj, qs)
        if masked:
            key = lax.broadcasted_iota(jnp.int32, s.shape, 0)
            qry = lax.broadcasted_iota(jnp.int32, s.shape, 1)
            s = jnp.where(_chunk_of(key) <= _chunk_of(qry & (t - 1)), s, NEG)
        store_scores(s, slot)

    def N():
        store_scores(_dot_nt(k_ref[0, 0:t, :], _stack_q(qn_ref[0])), SLOT_NEXT)

    def C(j, slot):
        vtj = vt_ref[0, :, pl.ds(pl.multiple_of(j * t, t), t)]
        _softmax_update_t(s_sc[slot], cm_sc[slot], vtj, m_sc, acc_sc)

    @pl.when(i == 0)
    def _():
        S(0, SLOT_A, True)
        N()
        C(0, SLOT_A)

    @pl.when(i == 1)
    def _():
        S(1, SLOT_A, True)
        C(0, SLOT_NEXT)
        N()
        C(1, SLOT_A)

    @pl.when(i >= 2)
    def _():
        S(1, SLOT_A)
        C(0, SLOT_NEXT)

        def pair(jj, c):
            j = 1 + 2 * jj
            S(j + 1, SLOT_B)
            C(j, SLOT_A)
            S(j + 2, SLOT_A)
            C(j + 1, SLOT_B)
            return c

        lax.fori_loop(0, (i - 2) // 2, pair, 0)

        @pl.when(i % 2 == 0)
        def _():
            S(i, SLOT_B, True)
            C(i - 1, SLOT_A)
            N()
            C(i, SLOT_B)

        @pl.when(i % 2 == 1)
        def _():
            S(i - 1, SLOT_B)
            C(i - 2, SLOT_A)
            S(i, SLOT_A, True)
            C(i - 1, SLOT_B)
            N()
            C(i, SLOT_A)

    lam = _lambda(lq1_ref[...], lk1_ref[...], lq2_ref[...], lk2_ref[...])
    on = acc_sc[0:V_DIM, :] * (1.0 / acc_sc[V_DIM:V_DIM + 1, :])
    o = on[:, :t] - lam * on[:, t:]
    inv = lax.rsqrt(jnp.mean(o * o, axis=0, keepdims=True) + EPS)
    o_ref[0] = (o * inv * (sub_ref[...] * (1.0 - LAMBDA_INIT))).astype(BF16)


def _attn_prompt(q, k, vt, subln, lq1, lk1, lq2, lk2, *, t):
    b, s, _ = q.shape
    nq = s // t
    small = lambda a: _resident(a.shape)
    subln = subln.reshape(V_DIM, 1)
    return pl.pallas_call(
        _attn_prompt_kernel,
        grid=(b, N_HEADS, nq),
        in_specs=[
            pl.BlockSpec((1, t, V_DIM), lambda bi, h, i: (bi, i, h)),
            pl.BlockSpec((1, t, V_DIM), lambda bi, h, i: (bi, jnp.minimum(i + 1, nq - 1), h)),
            pl.BlockSpec((1, s, V_DIM), lambda bi, h, i: (bi, 0, h)),
            pl.BlockSpec((1, V_DIM, s), lambda bi, h, i: (bi, h, 0)),
            small(subln), small(lq1), small(lk1), small(lq2), small(lk2),
        ],
        out_specs=pl.BlockSpec((1, V_DIM, t), lambda bi, h, i: (bi, h, i)),
        out_shape=jax.ShapeDtypeStruct((b, D_MODEL, s), BF16),
        scratch_shapes=[pltpu.VMEM((3, t, 2 * t), F32), pltpu.VMEM((3, 1, 2 * t), F32),
                        pltpu.VMEM((1, 2 * t), F32), pltpu.VMEM((V_DIM + ONES_ROWS, 2 * t), F32)],
        compiler_params=pltpu.CompilerParams(dimension_semantics=("arbitrary",) * 3, vmem_limit_bytes=VMEM_LIMIT),
        name="attn_prompt",
    )(q, q, k, vt, subln, lq1, lk1, lq2, lk2)


def _attn_sample_kernel(q_ref, kc_ref, vc_ref, kn_ref, vn_ref, sub_ref, lq1_ref, lk1_ref, lq2_ref, lk2_ref,
                        o_ref, m_sc, l_sc, acc_sc, *, q_start):
    t = q_ref.shape[1]
    tkb = kc_ref.shape[1] // N_HEADS
    j = pl.program_id(1)

    @pl.when(j == 0)
    def _():
        _softmax_init(m_sc, l_sc, acc_sc)

    head_cols = lambda h: slice(h * V_DIM, (h + 1) * V_DIM)
    for h in range(N_HEADS):
        qs = _stack_q(q_ref[0, :, head_cols(h)])
        kh = kc_ref[0, pl.ds(h, tkb, stride=N_HEADS), :].astype(BF16)
        vh = vc_ref[0, pl.ds(h, tkb, stride=N_HEADS), :].astype(BF16)
        _softmax_update(_dot_nt(qs, kh), vh, m_sc.at[h], l_sc.at[h], acc_sc.at[h])

    @pl.when(j == pl.num_programs(1) - 1)
    def _():
        lam = _lambda(lq1_ref[...], lk1_ref[...], lq2_ref[...], lk2_ref[...])
        for h in range(N_HEADS):
            qs = _stack_q(q_ref[0, :, head_cols(h)])
            s_n = _dot_nt(qs, kn_ref[0, :, head_cols(h)])
            row = lax.broadcasted_iota(jnp.int32, s_n.shape, 0)
            col = lax.broadcasted_iota(jnp.int32, s_n.shape, 1)
            s_n = jnp.where(_chunk_of(q_start + col) <= _chunk_of(q_start + (row & (t - 1))), s_n, NEG)
            m_prev = m_sc[h]
            m_fin = jnp.maximum(m_prev, jnp.max(s_n, axis=1, keepdims=True))
            alpha = jnp.exp2(m_prev - m_fin)
            p_n = jnp.exp2(s_n - m_fin[:, :t])
            l = jnp.sum(alpha * l_sc[h], axis=1, keepdims=True) + jnp.sum(p_n, axis=1, keepdims=True)
            acc = alpha * acc_sc[h] + _dot(p_n.astype(BF16), vn_ref[0, :, head_cols(h)])
            o_ref[0, :, head_cols(h)] = _finish(acc, l, t, lam, sub_ref[...])


def _attn_sample(q, cache_k, cache_v, kn, vn, subln, lq1, lk1, lq2, lk2, *, tkb):
    b, t, _ = q.shape
    past = cache_k.shape[1]
    kc = cache_k.reshape(b, past * N_HEADS, V_DIM)
    vc = cache_v.reshape(b, past * N_HEADS, V_DIM)
    small = lambda a: _resident(a.shape)
    rows = pl.BlockSpec((1, t, D_MODEL), lambda bi, j: (bi, 0, 0))
    blk = pl.BlockSpec((1, tkb * N_HEADS, V_DIM), lambda bi, j: (bi, j, 0))
    return pl.pallas_call(
        functools.partial(_attn_sample_kernel, q_start=past),
        grid=(b, past // tkb),
        in_specs=[rows, blk, blk, rows, rows, small(subln), small(lq1), small(lk1), small(lq2), small(lk2)],
        out_specs=rows,
        out_shape=jax.ShapeDtypeStruct((b, t, D_MODEL), BF16),
        scratch_shapes=[pltpu.VMEM((N_HEADS, 2 * t, LANES), F32), pltpu.VMEM((N_HEADS, 2 * t, LANES), F32),
                        pltpu.VMEM((N_HEADS, 2 * t, V_DIM), F32)],
        compiler_params=pltpu.CompilerParams(dimension_semantics=("arbitrary",) * 2, vmem_limit_bytes=VMEM_LIMIT),
        name="attn_sample",
    )(q, kc, vc, kn, vn, subln, lq1, lk1, lq2, lk2)


def _trunk(x, conv_state, cache, w, *, tm_ffn, tm_conv, t_attn):
    b, s, _ = x.shape
    n = b * s
    x2 = x.reshape(n, D_MODEL)
    (h,) = _ffn(x2, w["g_ffn1"][0], w["ffn_in"][0][0], w["ffn_out"][0][0], tm=tm_ffn)
    conv_args = (h.reshape(b, s, D_MODEL), conv_state, w["g_mix"][0], w["conv_in"], w["conv_w"], w["conv_out"])
    if s < tm_conv:
        h, new_conv = _conv_streams(*conv_args)
    else:
        h, new_conv = _conv(*conv_args, tm=tm_conv)
    x1, k_new, v_new, k_bf, v_bf = _ffn(h.reshape(n, D_MODEL), w["g_ffn2"][0], w["ffn_in"][0][1],
                                        w["ffn_out"][0][1], tm=tm_ffn, proj="kvt" if cache is None else "kv",
                                        proj_gain=w["g_kv"], proj_w=w["w_kv"], w_vt=w["w_vt"], seq=s)
    h1, q_bf = _ffn(x1, w["g_ffn1"][1], w["ffn_in"][1][0], w["ffn_out"][1][0], tm=tm_ffn, proj="q",
                    proj_gain=w["g_mix"][1], proj_w=w["w_q"])
    shp = (b, s, D_MODEL)
    lam = (w["lq1"], w["lk1"], w["lq2"], w["lk2"])
    if cache is None:
        o = _attn_prompt(q_bf.reshape(shp), k_bf.reshape(shp), v_bf, w["subln"], *lam, t=t_attn)
    else:
        o = _attn_sample(q_bf.reshape(shp), cache[0], cache[1], k_bf.reshape(shp), v_bf.reshape(shp),
                         w["subln"], *lam, tkb=min(1024, cache[0].shape[1])).reshape(n, D_MODEL)
    (y,) = _ffn(h1, w["g_ffn2"][1], w["ffn_in"][1][1], w["ffn_out"][1][1], tm=tm_ffn,
                pre=(o, w["w_o"]), final_gain=w["g_final"])
    kv_shape = (b, s, N_HEADS, V_DIM)
    return y.reshape(shp), new_conv, k_new.reshape(kv_shape), v_new.reshape(kv_shape)


def kernel(x_prompt, x_sample, state_conv, cache_k, cache_v, norm_ffn1, norm_mix, norm_ffn2, ffn_w_in, ffn_w_out,
           conv_w_in, conv_w, conv_w_out, norm_kv, w_k, w_v, w_q, lambda_q1, lambda_k1, lambda_q2, lambda_k2,
           subln, w_o, norm_final):
    depth = norm_ffn1.shape[0]
    row = lambda a: a.reshape(1, -1)
    w = {
        "g_ffn1": [row(norm_ffn1[i]) for i in range(depth)],
        "g_mix": [row(norm_mix[i]) for i in range(depth)],
        "g_ffn2": [row(norm_ffn2[i]) for i in range(depth)],
        "ffn_in": [[ffn_w_in[i, j].astype(BF16) for j in range(2)] for i in range(depth)],
        "ffn_out": [[ffn_w_out[i, j].astype(BF16) for j in range(2)] for i in range(depth)],
        "conv_in": conv_w_in[0].astype(BF16),
        "conv_w": conv_w[0],
        "conv_out": conv_w_out[0].astype(BF16),
        "g_kv": row(norm_kv),
        "w_kv": jnp.concatenate([w_k, w_v], axis=1).astype(BF16),
        "w_vt": w_v.T.astype(BF16),
        "w_q": w_q[0].astype(BF16),
        "lq1": row(lambda_q1[0]), "lk1": row(lambda_k1[0]), "lq2": row(lambda_q2[0]), "lk2": row(lambda_k2[0]),
        "subln": row(subln[0]),
        "w_o": w_o[0].astype(BF16),
        "g_final": row(norm_final),
    }
    b = x_prompt.shape[0]
    conv0 = jnp.zeros((b, 1, CONV_W - 1, D_MODEL), x_prompt.dtype)
    y_p, conv_p, k_p, v_p = _trunk(x_prompt, conv0, None, w, tm_ffn=512, tm_conv=512, t_attn=512)
    y_s, conv_s, k_s, v_s = _trunk(x_sample, state_conv, (cache_k, cache_v), w, tm_ffn=512, tm_conv=512,
                                   t_attn=None)
    return (y_p, y_s, conv_p, k_p, v_p, conv_s, k_s, v_s)
```

```python
import functools
import math

import jax
import jax.numpy as jnp
from jax import lax
from jax.experimental import pallas as pl
from jax.experimental.pallas import tpu as pltpu

D_MODEL = 1024
CHUNK = 64
CONV_W = 3
D_FF = 2816
N_HEADS = 8
HEAD_DIM = 64
V_DIM = 2 * HEAD_DIM
EPS = 1e-6
NEG = -1e30
LAMBDA_INIT = 0.8 - 0.6 * math.exp(-0.3 * 1)
LOG2E = math.log2(math.e)
Q_SCALE = HEAD_DIM ** -0.5 * LOG2E

V7X_VMEM_BYTES = 64 * 1024 * 1024
VMEM_LIMIT = V7X_VMEM_BYTES - 8 * 1024 * 1024
LANES = 128

BF16 = jnp.bfloat16
F32 = jnp.float32


def _resident(shape):
    nd = len(shape)
    return pl.BlockSpec(shape, lambda *_: (0,) * nd, pipeline_mode=pl.Buffered(1))


def _resident_slice(shape, lead):
    tail = tuple(shape[len(lead):])
    index = tuple(lead) + (0,) * len(tail)
    return pl.BlockSpec((None,) * len(lead) + tail, lambda *_: index, pipeline_mode=pl.Buffered(1))


def _rms(x, g):
    return x * lax.rsqrt(jnp.mean(x * x, axis=-1, keepdims=True) + EPS) * g


def _dot(a, b):
    return jnp.dot(a, b, preferred_element_type=F32)


def _dot_nt(a, b):
    return lax.dot_general(a, b, (((1,), (1,)), ((), ())), preferred_element_type=F32)


def _ffn_kernel(*refs, pre_proj, final_norm, proj):
    refs = list(refs)
    x_ref = refs.pop(0)
    if pre_proj:
        a_ref, wpre_ref = refs.pop(0), refs.pop(0)
    g_ref, win_ref, wout_ref = refs.pop(0), refs.pop(0), refs.pop(0)
    if final_norm:
        gf_ref = refs.pop(0)
    if proj is not None:
        gp_ref, wp_ref = refs.pop(0), refs.pop(0)
    y_ref = refs.pop(0)

    x = x_ref[...]
    if pre_proj == "rows":
        x = x + _dot(a_ref[...], wpre_ref[...])
    elif pre_proj == "cols":
        x = x + lax.dot_general(a_ref[0], wpre_ref[...], (((0,), (0,)), ((), ())), preferred_element_type=F32)
    xn = _rms(x, g_ref[...]).astype(BF16)
    gu = _dot(xn, win_ref[...])
    g, u = gu[:, :D_FF], gu[:, D_FF:]
    act = (g * (1.0 / (1.0 + jnp.exp(-g))) * u).astype(BF16)
    y = x + 0.5 * _dot(act, wout_ref[...])
    y_ref[...] = _rms(y, gf_ref[...]) if final_norm else y

    if proj in ("kv", "kvt"):
        kvn = _rms(y, gp_ref[...]).astype(BF16)
        r = _dot(kvn, wp_ref[...])
        k, v = r[:, :D_MODEL], r[:, D_MODEL:]
        if proj == "kv":
            kf_ref, vf_ref, kb_ref, vb_ref = refs
            vb_ref[...] = v.astype(BF16)
        else:
            kf_ref, vf_ref, kb_ref, vtb_ref = refs
            vtb_ref[0] = v.T.astype(BF16)
        kf_ref[...] = k
        vf_ref[...] = v
        kb_ref[...] = k.astype(BF16)
    elif proj == "q":
        (qb_ref,) = refs
        r = _dot(_rms(y, gp_ref[...]).astype(BF16), wp_ref[...])
        qb_ref[...] = (r * Q_SCALE).astype(BF16)


def _ffn(x, gain, w_in, w_out, layer, *, tm, pre=None, final_gain=None, proj=None, proj_gain=None, proj_w=None,
         seq=None):
    n = x.shape[0]
    tm = min(tm, n)
    row = lambda cols: pl.BlockSpec((tm, cols), lambda i: (i, 0))
    args, specs = [x], [row(D_MODEL)]
    pre_kind = None
    if pre is not None:
        a, w_pre = pre
        args += [a, w_pre]
        if a.ndim == 2:
            pre_kind = "rows"
            specs += [row(D_MODEL), _resident(w_pre.shape)]
        else:
            pre_kind = "cols"
            nsb_a = a.shape[2] // tm
            specs += [pl.BlockSpec((1, D_MODEL, tm), lambda i: (i // nsb_a, 0, i % nsb_a)), _resident(w_pre.shape)]
    args += [gain, w_in, w_out]
    specs += [_resident(gain.shape), _resident_slice(w_in.shape, layer), _resident_slice(w_out.shape, layer)]
    if final_gain is not None:
        args.append(final_gain)
        specs.append(_resident(final_gain.shape))
    out_shape = [jax.ShapeDtypeStruct((n, D_MODEL), F32)]
    out_specs = [row(D_MODEL)]
    if proj is not None:
        args += [proj_gain, proj_w]
        specs += [_resident(proj_gain.shape), _resident(proj_w.shape)]
        dts = {"kv": (F32, F32, BF16, BF16), "kvt": (F32, F32, BF16), "q": (BF16,)}[proj]
        out_shape += [jax.ShapeDtypeStruct((n, D_MODEL), dt) for dt in dts]
        out_specs += [row(D_MODEL) for _ in dts]
        if proj == "kvt":
            nsb = seq // tm
            out_shape.append(jax.ShapeDtypeStruct((n // seq, D_MODEL, seq), BF16))
            out_specs.append(pl.BlockSpec((1, D_MODEL, tm), lambda i: (i // nsb, 0, i % nsb)))
    return pl.pallas_call(
        functools.partial(_ffn_kernel, pre_proj=pre_kind, final_norm=final_gain is not None, proj=proj),
        grid=(n // tm,),
        in_specs=specs,
        out_specs=out_specs,
        out_shape=out_shape,
        compiler_params=pltpu.CompilerParams(dimension_semantics=("arbitrary",), vmem_limit_bytes=VMEM_LIMIT),
        name="ffn" + ("_pre" if pre is not None else "") + ("_" + proj if proj else "") + ("_fin" if final_gain is not None else ""),
    )(*args)


def _conv_kernel(x_ref, st_ref, g_ref, win_ref, cw_ref, wout_ref, o_ref, ns_ref, carry):
    tm = x_ref.shape[1]

    @pl.when(pl.program_id(1) == 0)
    def _():
        carry[0:CONV_W - 1, :] = st_ref[0, 0]

    h = x_ref[0]
    hn = _rms(h, g_ref[...]).astype(BF16)
    proj = _dot(hn, win_ref[...])
    b_g, c_g, hh = proj[:, :D_MODEL], proj[:, D_MODEL:2 * D_MODEL], proj[:, 2 * D_MODEL:]
    u = c_g * hh
    prev0, prev1 = carry[0:1, :], carry[1:2, :]
    row = lax.broadcasted_iota(jnp.int32, (tm, D_MODEL), 0)
    r1 = pltpu.roll(u, 1, 0)
    r2 = pltpu.roll(u, 2, 0)
    u1 = jnp.where(row == 0, prev1, r1)
    u2 = jnp.where(row == 0, prev0, jnp.where(row == 1, prev1, r2))
    cw = cw_ref[...]
    conv = cw[0:1, :] * u2 + cw[1:2, :] * u1 + cw[2:3, :] * u
    o_ref[0] = h + _dot((b_g * conv).astype(BF16), wout_ref[...])
    tail = r2[0:CONV_W - 1, :]
    carry[0:CONV_W - 1, :] = tail
    ns_ref[0, 0] = tail


def _conv_streams_kernel(x_ref, st_ref, g_ref, win_ref, cw_ref, wout_ref, o_ref, ns_ref, *, s):
    n = x_ref.shape[0]
    b = n // s
    h = x_ref[...]
    hn = _rms(h, g_ref[...]).astype(BF16)
    proj = _dot(hn, win_ref[...])
    b_g, c_g, hh = proj[:, :D_MODEL], proj[:, D_MODEL:2 * D_MODEL], proj[:, 2 * D_MODEL:]
    u = c_g * hh
    st = st_ref[...]
    per_row = lambda rows: jnp.broadcast_to(rows, (b, s, D_MODEL)).reshape(n, D_MODEL)
    prev0, prev1 = per_row(st[:, 0:1, :]), per_row(st[:, 1:2, :])
    pos = lax.broadcasted_iota(jnp.int32, (n, D_MODEL), 0) & (s - 1)
    u1 = jnp.where(pos == 0, prev1, pltpu.roll(u, 1, 0))
    u2 = jnp.where(pos == 0, prev0, jnp.where(pos == 1, prev1, pltpu.roll(u, 2, 0)))
    cw = cw_ref[...]
    conv = cw[0:1, :] * u2 + cw[1:2, :] * u1 + cw[2:3, :] * u
    o_ref[...] = h + _dot((b_g * conv).astype(BF16), wout_ref[...])
    ns_ref[...] = pltpu.roll(u, n - (s - (CONV_W - 1)), 0).reshape(b, s, D_MODEL)[:, 0:CONV_W - 1, :]


def _conv_streams(x, state, gain, w_in, conv_w, w_out):
    b, s, _ = x.shape
    assert s & (s - 1) == 0 and s % 8 == 0, s
    n = b * s
    full = lambda shape: pl.BlockSpec(shape, lambda i: (0,) * len(shape))
    y, ns = pl.pallas_call(
        functools.partial(_conv_streams_kernel, s=s),
        grid=(1,),
        in_specs=[full((n, D_MODEL)), full((b, CONV_W - 1, D_MODEL)),
                  full(gain.shape), full(w_in.shape), full(conv_w.shape), full(w_out.shape)],
        out_specs=[full((n, D_MODEL)), full((b, CONV_W - 1, D_MODEL))],
        out_shape=[jax.ShapeDtypeStruct((n, D_MODEL), F32), jax.ShapeDtypeStruct((b, CONV_W - 1, D_MODEL), F32)],
        compiler_params=pltpu.CompilerParams(dimension_semantics=("arbitrary",), vmem_limit_bytes=VMEM_LIMIT),
        name="conv_streams",
    )(x.reshape(n, D_MODEL), state.reshape(b, CONV_W - 1, D_MODEL), gain, w_in, conv_w, w_out)
    return y.reshape(b, s, D_MODEL), ns.reshape(b, 1, CONV_W - 1, D_MODEL)


def _conv(x, state, gain, w_in, conv_w, w_out, *, tm):
    b, s, _ = x.shape
    tm = min(tm, s)
    return pl.pallas_call(
        _conv_kernel,
        grid=(b, s // tm),
        in_specs=[
            pl.BlockSpec((1, tm, D_MODEL), lambda bi, i: (bi, i, 0)),
            pl.BlockSpec((1, 1, CONV_W - 1, D_MODEL), lambda bi, i: (bi, 0, 0, 0)),
            _resident(gain.shape), _resident(w_in.shape), _resident(conv_w.shape), _resident(w_out.shape),
        ],
        out_specs=[
            pl.BlockSpec((1, tm, D_MODEL), lambda bi, i: (bi, i, 0)),
            pl.BlockSpec((1, 1, CONV_W - 1, D_MODEL), lambda bi, i: (bi, 0, 0, 0)),
        ],
        out_shape=[jax.ShapeDtypeStruct((b, s, D_MODEL), F32),
                   jax.ShapeDtypeStruct((b, 1, CONV_W - 1, D_MODEL), F32)],
        scratch_shapes=[pltpu.VMEM((8, D_MODEL), F32)],
        compiler_params=pltpu.CompilerParams(dimension_semantics=("arbitrary", "arbitrary"),
                                             vmem_limit_bytes=VMEM_LIMIT),
        name="conv",
    )(x, state, gain, w_in, conv_w, w_out)


def _stack_q(q):
    lane = lax.broadcasted_iota(jnp.int32, q.shape, 1)
    zero = jnp.zeros_like(q)
    return jnp.concatenate([jnp.where(lane < HEAD_DIM, q, zero), jnp.where(lane >= HEAD_DIM, q, zero)], axis=0)


def _chunk_of(pos):
    return lax.shift_right_logical(pos, CHUNK.bit_length() - 1)


def _lambda(lq1, lk1, lq2, lk2):
    return (jnp.exp(jnp.sum(lq1 * lk1, axis=-1, keepdims=True))
            - jnp.exp(jnp.sum(lq2 * lk2, axis=-1, keepdims=True)) + LAMBDA_INIT)


def _finish(acc, l, t, lam, subln):
    o = acc[:t] / l[:t] - lam * (acc[t:] / l[t:])
    return (_rms(o, subln) * (1.0 - LAMBDA_INIT)).astype(BF16)


def _softmax_init(m_ref, l_ref, acc_ref):
    m_ref[...] = jnp.full(m_ref.shape, NEG, F32)
    l_ref[...] = jnp.zeros(l_ref.shape, F32)
    acc_ref[...] = jnp.zeros(acc_ref.shape, F32)


def _softmax_update(s, v, m_ref, l_ref, acc_ref):
    cols = [s[:, c * LANES:(c + 1) * LANES] for c in range(s.shape[1] // LANES)]
    m_prev = m_ref[...]
    m_next = jnp.maximum(m_prev, jnp.max(functools.reduce(jnp.maximum, cols), axis=1, keepdims=True))
    alpha = jnp.exp2(m_prev - m_next)
    ps = [jnp.exp2(c - m_next) for c in cols]
    l_ref[...] = alpha * l_ref[...] + functools.reduce(jnp.add, ps)
    p = jnp.concatenate([c.astype(BF16) for c in ps], axis=1)
    acc_ref[...] = alpha * acc_ref[...] + _dot(p, v)
    m_ref[...] = m_next


ONES_ROWS = 16


def _softmax_update_t(s, cmax, vt, m_ref, acc_ref):
    m_prev = m_ref[...]
    m_next = jnp.maximum(m_prev, cmax)
    alpha = jnp.exp2(m_prev - m_next)
    p = jnp.exp2(s - m_next).astype(BF16)
    vt1 = jnp.concatenate([vt, jnp.ones((ONES_ROWS, vt.shape[1]), BF16)], axis=0)
    acc_ref[...] = alpha * acc_ref[...] + _dot(vt1, p)
    m_ref[...] = m_next


SLOT_A, SLOT_B, SLOT_NEXT = 0, 1, 2


def _attn_prompt_kernel(q_ref, qn_ref, k_ref, vt_ref, sub_ref, lq1_ref, lk1_ref, lq2_ref, lk2_ref, o_ref,
                        s_sc, cm_sc, m_sc, acc_sc):
    t = q_ref.shape[1]
    i = pl.program_id(2)
    qs = _stack_q(q_ref[0])
    m_sc[...] = jnp.full(m_sc.shape, NEG, F32)
    acc_sc[...] = jnp.zeros(acc_sc.shape, F32)

    def store_scores(s, slot):
        s_sc[slot] = s
        cm_sc[slot] = jnp.max(s, axis=0, keepdims=True)

    def S(j, slot, masked=False):
        kj = k_ref[0, pl.ds(pl.multiple_of(j * t, t), t), :]
        s = _dot_nt(kj, qs)
        if masked:
            key = lax.broadcasted_iota(jnp.int32, s.shape, 0)
            qry = lax.broadcasted_iota(jnp.int32, s.shape, 1)
            s = jnp.where(_chunk_of(key) <= _chunk_of(qry & (t - 1)), s, NEG)
        store_scores(s, slot)

    def N():
        store_scores(_dot_nt(k_ref[0, 0:t, :], _stack_q(qn_ref[0])), SLOT_NEXT)

    def C(j, slot):
        vtj = vt_ref[0, :, pl.ds(pl.multiple_of(j * t, t), t)]
        _softmax_update_t(s_sc[slot], cm_sc[slot], vtj, m_sc, acc_sc)

    @pl.when(i == 0)
    def _():
        S(0, SLOT_A, True)
        N()
        C(0, SLOT_A)

    @pl.when(i == 1)
    def _():
        S(1, SLOT_A, True)
        C(0, SLOT_NEXT)
        N()
        C(1, SLOT_A)

    @pl.when(i >= 2)
    def _():
        S(1, SLOT_A)
        C(0, SLOT_NEXT)

        def pair(j):
            S(j + 1, SLOT_B)
            C(j, SLOT_A)
            S(j + 2, SLOT_A)
            C(j + 1, SLOT_B)

        def quad(jj, c):
            pair(1 + 4 * jj)
            pair(3 + 4 * jj)
            return c

        def single(jj, c):
            pair(1 + 2 * jj)
            return c

        npairs = (i - 2) // 2
        lax.fori_loop(0, npairs // 2, quad, 0)
        lax.fori_loop(npairs - npairs % 2, npairs, single, 0)

        @pl.when(i % 2 == 0)
        def _():
            S(i, SLOT_B, True)
            C(i - 1, SLOT_A)
            N()
            C(i, SLOT_B)

        @pl.when(i % 2 == 1)
        def _():
            S(i - 1, SLOT_B)
            C(i - 2, SLOT_A)
            S(i, SLOT_A, True)
            C(i - 1, SLOT_B)
            N()
            C(i, SLOT_A)

    lam = _lambda(lq1_ref[...], lk1_ref[...], lq2_ref[...], lk2_ref[...])
    on = acc_sc[0:V_DIM, :] * (1.0 / acc_sc[V_DIM:V_DIM + 1, :])
    o = on[:, :t] - lam * on[:, t:]
    inv = lax.rsqrt(jnp.mean(o * o, axis=0, keepdims=True) + EPS)
    o_ref[0] = (o * inv * (sub_ref[...] * (1.0 - LAMBDA_INIT))).astype(BF16)


def _attn_prompt(q, k, vt, subln, lq1, lk1, lq2, lk2, *, t):
    b, s, _ = q.shape
    nq = s // t
    small = lambda a: _resident(a.shape)
    subln = subln.reshape(V_DIM, 1)
    return pl.pallas_call(
        _attn_prompt_kernel,
        grid=(b, N_HEADS, nq),
        in_specs=[
            pl.BlockSpec((1, t, V_DIM), lambda bi, h, i: (bi, i, h)),
            pl.BlockSpec((1, t, V_DIM), lambda bi, h, i: (bi, jnp.minimum(i + 1, nq - 1), h)),
            pl.BlockSpec((1, s, V_DIM), lambda bi, h, i: (bi, 0, h)),
            pl.BlockSpec((1, V_DIM, s), lambda bi, h, i: (bi, h, 0)),
            small(subln), small(lq1), small(lk1), small(lq2), small(lk2),
        ],
        out_specs=pl.BlockSpec((1, V_DIM, t), lambda bi, h, i: (bi, h, i)),
        out_shape=jax.ShapeDtypeStruct((b, D_MODEL, s), BF16),
        scratch_shapes=[pltpu.VMEM((3, t, 2 * t), F32), pltpu.VMEM((3, 1, 2 * t), F32),
                        pltpu.VMEM((1, 2 * t), F32), pltpu.VMEM((V_DIM + ONES_ROWS, 2 * t), F32)],
        compiler_params=pltpu.CompilerParams(dimension_semantics=("arbitrary",) * 3, vmem_limit_bytes=VMEM_LIMIT),
        name="attn_prompt",
    )(q, q, k, vt, subln, lq1, lk1, lq2, lk2)


def _attn_sample_kernel(q_ref, kc_ref, vc_ref, kn_ref, vn_ref, sub_ref, lq1_ref, lk1_ref, lq2_ref, lk2_ref,
                        o_ref, m_sc, l_sc, acc_sc, *, q_start):
    t = q_ref.shape[1]
    tkb = kc_ref.shape[1] // N_HEADS
    j = pl.program_id(1)

    @pl.when(j == 0)
    def _():
        _softmax_init(m_sc, l_sc, acc_sc)

    head_cols = lambda h: slice(h * V_DIM, (h + 1) * V_DIM)
    for h in range(N_HEADS):
        qs = _stack_q(q_ref[0, :, head_cols(h)])
        kh = kc_ref[0, pl.ds(h, tkb, stride=N_HEADS), :].astype(BF16)
        vh = vc_ref[0, pl.ds(h, tkb, stride=N_HEADS), :].astype(BF16)
        _softmax_update(_dot_nt(qs, kh), vh, m_sc.at[h], l_sc.at[h], acc_sc.at[h])

    @pl.when(j == pl.num_programs(1) - 1)
    def _():
        lam = _lambda(lq1_ref[...], lk1_ref[...], lq2_ref[...], lk2_ref[...])
        for h in range(N_HEADS):
            qs = _stack_q(q_ref[0, :, head_cols(h)])
            s_n = _dot_nt(qs, kn_ref[0, :, head_cols(h)])
            row = lax.broadcasted_iota(jnp.int32, s_n.shape, 0)
            col = lax.broadcasted_iota(jnp.int32, s_n.shape, 1)
            s_n = jnp.where(_chunk_of(q_start + col) <= _chunk_of(q_start + (row & (t - 1))), s_n, NEG)
            m_prev = m_sc[h]
            m_fin = jnp.maximum(m_prev, jnp.max(s_n, axis=1, keepdims=True))
            alpha = jnp.exp2(m_prev - m_fin)
            p_n = jnp.exp2(s_n - m_fin[:, :t])
            l = jnp.sum(alpha * l_sc[h], axis=1, keepdims=True) + jnp.sum(p_n, axis=1, keepdims=True)
            acc = alpha * acc_sc[h] + _dot(p_n.astype(BF16), vn_ref[0, :, head_cols(h)])
            o_ref[0, :, head_cols(h)] = _finish(acc, l, t, lam, sub_ref[...])


def _attn_sample(q, cache_k, cache_v, kn, vn, subln, lq1, lk1, lq2, lk2, *, tkb):
    b, t, _ = q.shape
    past = cache_k.shape[1]
    kc = cache_k.reshape(b, past * N_HEADS, V_DIM)
    vc = cache_v.reshape(b, past * N_HEADS, V_DIM)
    small = lambda a: _resident(a.shape)
    rows = pl.BlockSpec((1, t, D_MODEL), lambda bi, j: (bi, 0, 0))
    blk = pl.BlockSpec((1, tkb * N_HEADS, V_DIM), lambda bi, j: (bi, j, 0))
    return pl.pallas_call(
        functools.partial(_attn_sample_kernel, q_start=past),
        grid=(b, past // tkb),
        in_specs=[rows, blk, blk, rows, rows, small(subln), small(lq1), small(lk1), small(lq2), small(lk2)],
        out_specs=rows,
        out_shape=jax.ShapeDtypeStruct((b, t, D_MODEL), BF16),
        scratch_shapes=[pltpu.VMEM((N_HEADS, 2 * t, LANES), F32), pltpu.VMEM((N_HEADS, 2 * t, LANES), F32),
                        pltpu.VMEM((N_HEADS, 2 * t, V_DIM), F32)],
        compiler_params=pltpu.CompilerParams(dimension_semantics=("arbitrary",) * 2, vmem_limit_bytes=VMEM_LIMIT),
        name="attn_sample",
    )(q, kc, vc, kn, vn, subln, lq1, lk1, lq2, lk2)


def _trunk(x, conv_state, cache, w, *, tm_ffn, tm_conv, t_attn):
    b, s, _ = x.shape
    n = b * s
    x2 = x.reshape(n, D_MODEL)
    (h,) = _ffn(x2, w["g_ffn1"][0], w["ffn_in"], w["ffn_out"], (0, 0), tm=tm_ffn)
    conv_args = (h.reshape(b, s, D_MODEL), conv_state, w["g_mix"][0], w["conv_in"], w["conv_w"], w["conv_out"])
    if s < tm_conv:
        h, new_conv = _conv_streams(*conv_args)
    else:
        h, new_conv = _conv(*conv_args, tm=tm_conv)
    x1, k_new, v_new, k_bf, v_bf = _ffn(h.reshape(n, D_MODEL), w["g_ffn2"][0], w["ffn_in"],
                                        w["ffn_out"], (0, 1), tm=tm_ffn, proj="kvt" if cache is None else "kv",
                                        proj_gain=w["g_kv"], proj_w=w["w_kv"], seq=s)
    h1, q_bf = _ffn(x1, w["g_ffn1"][1], w["ffn_in"], w["ffn_out"], (1, 0), tm=tm_ffn, proj="q",
                    proj_gain=w["g_mix"][1], proj_w=w["w_q"])
    shp = (b, s, D_MODEL)
    lam = (w["lq1"], w["lk1"], w["lq2"], w["lk2"])
    if cache is None:
        o = _attn_prompt(q_bf.reshape(shp), k_bf.reshape(shp), v_bf, w["subln"], *lam, t=t_attn)
    else:
        o = _attn_sample(q_bf.reshape(shp), cache[0], cache[1], k_bf.reshape(shp), v_bf.reshape(shp),
                         w["subln"], *lam, tkb=min(2048, cache[0].shape[1])).reshape(n, D_MODEL)
    (y,) = _ffn(h1, w["g_ffn2"][1], w["ffn_in"], w["ffn_out"], (1, 1), tm=tm_ffn,
                pre=(o, w["w_o"]), final_gain=w["g_final"])
    kv_shape = (b, s, N_HEADS, V_DIM)
    return y.reshape(shp), new_conv, k_new.reshape(kv_shape), v_new.reshape(kv_shape)


def kernel(x_prompt, x_sample, state_conv, cache_k, cache_v, norm_ffn1, norm_mix, norm_ffn2, ffn_w_in, ffn_w_out,
           conv_w_in, conv_w, conv_w_out, norm_kv, w_k, w_v, w_q, lambda_q1, lambda_k1, lambda_q2, lambda_k2,
           subln, w_o, norm_final):
    depth = norm_ffn1.shape[0]
    row = lambda a: a.reshape(1, -1)
    w = {
        "g_ffn1": [row(norm_ffn1[i]) for i in range(depth)],
        "g_mix": [row(norm_mix[i]) for i in range(depth)],
        "g_ffn2": [row(norm_ffn2[i]) for i in range(depth)],
        "ffn_in": ffn_w_in.astype(BF16),
        "ffn_out": ffn_w_out.astype(BF16),
        "conv_in": conv_w_in[0].astype(BF16),
        "conv_w": conv_w[0],
        "conv_out": conv_w_out[0].astype(BF16),
        "g_kv": row(norm_kv),
        "w_kv": jnp.concatenate([w_k, w_v], axis=1).astype(BF16),
        "w_q": w_q[0].astype(BF16),
        "lq1": row(lambda_q1[0]), "lk1": row(lambda_k1[0]), "lq2": row(lambda_q2[0]), "lk2": row(lambda_k2[0]),
        "subln": row(subln[0]),
        "w_o": w_o[0].astype(BF16),
        "g_final": row(norm_final),
    }
    b = x_prompt.shape[0]
    conv0 = jnp.zeros((b, 1, CONV_W - 1, D_MODEL), x_prompt.dtype)
    y_p, conv_p, k_p, v_p = _trunk(x_prompt, conv0, None, w, tm_ffn=512, tm_conv=512, t_attn=512)
    y_s, conv_s, k_s, v_s = _trunk(x_sample, state_conv, (cache_k, cache_v), w, tm_ffn=512, tm_conv=512,
                                   t_attn=None)
    return (y_p, y_s, conv_p, k_p, v_p, conv_s, k_s, v_s)
```

```python
import functools
import math

import jax
import jax.numpy as jnp
from jax import lax
from jax.experimental import pallas as pl
from jax.experimental.pallas import tpu as pltpu

D_MODEL = 1024
CHUNK = 64
CONV_W = 3
D_FF = 2816
N_HEADS = 8
HEAD_DIM = 64
V_DIM = 2 * HEAD_DIM
EPS = 1e-6
NEG = -1e30
LAMBDA_INIT = 0.8 - 0.6 * math.exp(-0.3 * 1)
LOG2E = math.log2(math.e)
Q_SCALE = HEAD_DIM ** -0.5 * LOG2E

V7X_VMEM_BYTES = 64 * 1024 * 1024
VMEM_LIMIT = V7X_VMEM_BYTES - 8 * 1024 * 1024
LANES = 128

BF16 = jnp.bfloat16
F32 = jnp.float32


def _resident(shape):
    nd = len(shape)
    return pl.BlockSpec(shape, lambda *_: (0,) * nd, pipeline_mode=pl.Buffered(1))


def _resident_slice(shape, lead):
    tail = tuple(shape[len(lead):])
    index = tuple(lead) + (0,) * len(tail)
    return pl.BlockSpec((None,) * len(lead) + tail, lambda *_: index, pipeline_mode=pl.Buffered(1))


def _rms(x, g):
    return x * lax.rsqrt(jnp.mean(x * x, axis=-1, keepdims=True) + EPS) * g


def _dot(a, b):
    return jnp.dot(a, b, preferred_element_type=F32)


def _dot_nt(a, b):
    return lax.dot_general(a, b, (((1,), (1,)), ((), ())), preferred_element_type=F32)


def _ffn_kernel(*refs, nb_main, has_tail, v_cols, pre, final_norm, proj):
    refs = list(refs)
    take = lambda k: [refs.pop(0) for _ in range(k)]
    n_sets = 2 if has_tail else 1
    x_refs = take(n_sets)
    if pre:
        a_refs, (wpre_ref,) = take(n_sets), take(1)
    g_ref, win_ref, wout_ref = take(3)
    if final_norm:
        (gf_ref,) = take(1)
    if proj is not None:
        gp_ref, wp_ref = take(2)
    y_refs = take(n_sets)
    n_proj_out = {None: 0, "kv": 4, "q": 1}[proj]
    p_refs = [take(n_proj_out) for _ in range(n_sets)]

    is_tail = pl.program_id(0) == nb_main
    pre_main = lambda: x_refs[0][...] + lax.dot_general(a_refs[0][0], wpre_ref[...], (((0,), (0,)), ((), ())),
                                                        preferred_element_type=F32)
    if not has_tail:
        x = pre_main() if pre else x_refs[0][...]
    elif pre:
        (x_sc,) = refs

        @pl.when(jnp.logical_not(is_tail))
        def _():
            x_sc[...] = pre_main()

        @pl.when(is_tail)
        def _():
            x_sc[...] = x_refs[1][...] + _dot(a_refs[1][...], wpre_ref[...])

        x = x_sc[...]
    else:
        x = jnp.where(is_tail, x_refs[1][...], x_refs[0][...])

    xn = _rms(x, g_ref[...]).astype(BF16)
    gu = _dot(xn, win_ref[...])
    g, u = gu[:, :D_FF], gu[:, D_FF:]
    act = (g * (1.0 / (1.0 + jnp.exp(-g))) * u).astype(BF16)
    y = x + 0.5 * _dot(act, wout_ref[...])
    y_out = _rms(y, gf_ref[...]) if final_norm else y
    if proj is not None:
        r = _dot(_rms(y, gp_ref[...]).astype(BF16), wp_ref[...])

    def write(y_ref, refs_, cols):
        y_ref[...] = y_out
        if proj == "kv":
            k, v = r[:, :D_MODEL], r[:, D_MODEL:]
            kf_ref, vf_ref, kb_ref, vb_ref = refs_
            kf_ref[...] = k
            vf_ref[...] = v
            kb_ref[...] = k.astype(BF16)
            if cols:
                vb_ref[0] = v.T.astype(BF16)
            else:
                vb_ref[...] = v.astype(BF16)
        elif proj == "q":
            refs_[0][...] = (r * Q_SCALE).astype(BF16)

    if has_tail:
        pl.when(jnp.logical_not(is_tail))(lambda: write(y_refs[0], p_refs[0], v_cols))
        pl.when(is_tail)(lambda: write(y_refs[1], p_refs[1], False))
    else:
        write(y_refs[0], p_refs[0], v_cols)


def _ffn(x_main, x_tail, gain, w_in, w_out, layer, *, tm, seq, pre=None, final_gain=None, proj=None,
         proj_gain=None, proj_w=None, v_cols=False):
    has_tail = x_tail is not None
    nb_main = x_main.shape[0] // tm
    nsb = seq // tm
    assert x_main.shape[0] == nb_main * tm and seq == nsb * tm, (x_main.shape, tm, seq)
    assert not has_tail or x_tail.shape[0] == tm, (x_tail.shape, tm)
    main_i = lambda i: jnp.minimum(i, nb_main - 1)
    main_rows = pl.BlockSpec((tm, D_MODEL), lambda i: (main_i(i), 0))
    main_cols = pl.BlockSpec((1, D_MODEL, tm), lambda i: (main_i(i) // nsb, 0, main_i(i) % nsb))
    tail_rows = pl.BlockSpec((tm, D_MODEL), lambda i: (0, 0), pipeline_mode=pl.Buffered(1))
    rows_like = lambda a, dt: jax.ShapeDtypeStruct(a.shape, dt)
    sets = [(x_main, main_rows)] + ([(x_tail, tail_rows)] if has_tail else [])

    args, specs = [x for x, _ in sets], [spec for _, spec in sets]
    scratch = []
    if pre is not None:
        a_main, a_tail, w_pre = pre
        args += [a_main] + ([a_tail] if has_tail else []) + [w_pre]
        specs += [main_cols] + ([tail_rows] if has_tail else []) + [_resident(w_pre.shape)]
        if has_tail:
            scratch.append(pltpu.VMEM((tm, D_MODEL), F32))
    args += [gain, w_in, w_out]
    specs += [_resident(gain.shape), _resident_slice(w_in.shape, layer), _resident_slice(w_out.shape, layer)]
    if final_gain is not None:
        args.append(final_gain)
        specs.append(_resident(final_gain.shape))
    if proj is not None:
        args += [proj_gain, proj_w]
        specs += [_resident(proj_gain.shape), _resident(proj_w.shape)]
    out_shape = [rows_like(x, F32) for x, _ in sets]
    out_specs = [spec for _, spec in sets]
    for k, (x, spec) in enumerate(sets):
        if proj == "kv":
            out_shape += [rows_like(x, F32), rows_like(x, F32), rows_like(x, BF16)]
            out_specs += [spec] * 3
            if v_cols and k == 0:
                out_shape.append(jax.ShapeDtypeStruct((x.shape[0] // seq, D_MODEL, seq), BF16))
                out_specs.append(main_cols)
            else:
                out_shape.append(rows_like(x, BF16))
                out_specs.append(spec)
        elif proj == "q":
            out_shape.append(rows_like(x, BF16))
            out_specs.append(spec)
    n_sets = len(sets)
    n_proj_out = (len(out_shape) - n_sets) // n_sets
    outs = pl.pallas_call(
        functools.partial(_ffn_kernel, nb_main=nb_main, has_tail=has_tail, v_cols=v_cols, pre=pre is not None,
                          final_norm=final_gain is not None, proj=proj),
        grid=(nb_main + n_sets - 1,),
        in_specs=specs,
        out_specs=out_specs,
        out_shape=out_shape,
        scratch_shapes=scratch,
        compiler_params=pltpu.CompilerParams(dimension_semantics=("arbitrary",), vmem_limit_bytes=VMEM_LIMIT),
        name="ffn" + ("_pre" if pre is not None else "") + ("_" + proj if proj else "") + ("_fin" if final_gain is not None else ""),
    )(*args)
    ys, ps = outs[:n_sets], outs[n_sets:]
    if not has_tail:
        return ys[0], None, ps, None
    return ys[0], ys[1], ps[:n_proj_out], ps[n_proj_out:]


def _conv_kernel(x_ref, st_ref, g_ref, win_ref, cw_ref, wout_ref, o_ref, ns_ref, carry):
    tm = x_ref.shape[1]

    @pl.when(pl.program_id(1) == 0)
    def _():
        carry[0:CONV_W - 1, :] = st_ref[0, 0]

    h = x_ref[0]
    hn = _rms(h, g_ref[...]).astype(BF16)
    proj = _dot(hn, win_ref[...])
    b_g, c_g, hh = proj[:, :D_MODEL], proj[:, D_MODEL:2 * D_MODEL], proj[:, 2 * D_MODEL:]
    u = c_g * hh
    prev0, prev1 = carry[0:1, :], carry[1:2, :]
    row = lax.broadcasted_iota(jnp.int32, (tm, D_MODEL), 0)
    r1 = pltpu.roll(u, 1, 0)
    r2 = pltpu.roll(u, 2, 0)
    u1 = jnp.where(row == 0, prev1, r1)
    u2 = jnp.where(row == 0, prev0, jnp.where(row == 1, prev1, r2))
    cw = cw_ref[...]
    conv = cw[0:1, :] * u2 + cw[1:2, :] * u1 + cw[2:3, :] * u
    o_ref[0] = h + _dot((b_g * conv).astype(BF16), wout_ref[...])
    tail = r2[0:CONV_W - 1, :]
    carry[0:CONV_W - 1, :] = tail
    ns_ref[0, 0] = tail


def _conv_streams_kernel(x_ref, st_ref, g_ref, win_ref, cw_ref, wout_ref, o_ref, ns_ref, *, s):
    n = x_ref.shape[0]
    b = n // s
    h = x_ref[...]
    hn = _rms(h, g_ref[...]).astype(BF16)
    proj = _dot(hn, win_ref[...])
    b_g, c_g, hh = proj[:, :D_MODEL], proj[:, D_MODEL:2 * D_MODEL], proj[:, 2 * D_MODEL:]
    u = c_g * hh
    st = st_ref[...]
    per_row = lambda rows: jnp.broadcast_to(rows, (b, s, D_MODEL)).reshape(n, D_MODEL)
    prev0, prev1 = per_row(st[:, 0:1, :]), per_row(st[:, 1:2, :])
    pos = lax.broadcasted_iota(jnp.int32, (n, D_MODEL), 0) & (s - 1)
    u1 = jnp.where(pos == 0, prev1, pltpu.roll(u, 1, 0))
    u2 = jnp.where(pos == 0, prev0, jnp.where(pos == 1, prev1, pltpu.roll(u, 2, 0)))
    cw = cw_ref[...]
    conv = cw[0:1, :] * u2 + cw[1:2, :] * u1 + cw[2:3, :] * u
    o_ref[...] = h + _dot((b_g * conv).astype(BF16), wout_ref[...])
    ns_ref[...] = pltpu.roll(u, n - (s - (CONV_W - 1)), 0).reshape(b, s, D_MODEL)[:, 0:CONV_W - 1, :]


def _conv_streams(x, state, gain, w_in, conv_w, w_out):
    b, s, _ = x.shape
    assert s & (s - 1) == 0 and s % 8 == 0, s
    n = b * s
    full = lambda shape: pl.BlockSpec(shape, lambda i: (0,) * len(shape))
    y, ns = pl.pallas_call(
        functools.partial(_conv_streams_kernel, s=s),
        grid=(1,),
        in_specs=[full((n, D_MODEL)), full((b, CONV_W - 1, D_MODEL)),
                  full(gain.shape), full(w_in.shape), full(conv_w.shape), full(w_out.shape)],
        out_specs=[full((n, D_MODEL)), full((b, CONV_W - 1, D_MODEL))],
        out_shape=[jax.ShapeDtypeStruct((n, D_MODEL), F32), jax.ShapeDtypeStruct((b, CONV_W - 1, D_MODEL), F32)],
        compiler_params=pltpu.CompilerParams(dimension_semantics=("arbitrary",), vmem_limit_bytes=VMEM_LIMIT),
        name="conv_streams",
    )(x.reshape(n, D_MODEL), state.reshape(b, CONV_W - 1, D_MODEL), gain, w_in, conv_w, w_out)
    return y.reshape(b, s, D_MODEL), ns.reshape(b, 1, CONV_W - 1, D_MODEL)


def _conv(x, state, gain, w_in, conv_w, w_out, *, tm):
    b, s, _ = x.shape
    tm = min(tm, s)
    return pl.pallas_call(
        _conv_kernel,
        grid=(b, s // tm),
        in_specs=[
            pl.BlockSpec((1, tm, D_MODEL), lambda bi, i: (bi, i, 0)),
            pl.BlockSpec((1, 1, CONV_W - 1, D_MODEL), lambda bi, i: (bi, 0, 0, 0)),
            _resident(gain.shape), _resident(w_in.shape), _resident(conv_w.shape), _resident(w_out.shape),
        ],
        out_specs=[
            pl.BlockSpec((1, tm, D_MODEL), lambda bi, i: (bi, i, 0)),
            pl.BlockSpec((1, 1, CONV_W - 1, D_MODEL), lambda bi, i: (bi, 0, 0, 0)),
        ],
        out_shape=[jax.ShapeDtypeStruct((b, s, D_MODEL), F32),
                   jax.ShapeDtypeStruct((b, 1, CONV_W - 1, D_MODEL), F32)],
        scratch_shapes=[pltpu.VMEM((8, D_MODEL), F32)],
        compiler_params=pltpu.CompilerParams(dimension_semantics=("arbitrary", "arbitrary"),
                                             vmem_limit_bytes=VMEM_LIMIT),
        name="conv",
    )(x, state, gain, w_in, conv_w, w_out)


def _stack_q(q):
    lane = lax.broadcasted_iota(jnp.int32, q.shape, 1)
    zero = jnp.zeros_like(q)
    return jnp.concatenate([jnp.where(lane < HEAD_DIM, q, zero), jnp.where(lane >= HEAD_DIM, q, zero)], axis=0)


def _chunk_of(pos):
    return lax.shift_right_logical(pos, CHUNK.bit_length() - 1)


def _lambda(lq1, lk1, lq2, lk2):
    return (jnp.exp(jnp.sum(lq1 * lk1, axis=-1, keepdims=True))
            - jnp.exp(jnp.sum(lq2 * lk2, axis=-1, keepdims=True)) + LAMBDA_INIT)


def _finish(acc, l, t, lam, subln):
    o = acc[:t] / l[:t] - lam * (acc[t:] / l[t:])
    return (_rms(o, subln) * (1.0 - LAMBDA_INIT)).astype(BF16)


def _softmax_init(m_ref, l_ref, acc_ref):
    m_ref[...] = jnp.full(m_ref.shape, NEG, F32)
    l_ref[...] = jnp.zeros(l_ref.shape, F32)
    acc_ref[...] = jnp.zeros(acc_ref.shape, F32)


def _softmax_update(s, v, m_ref, l_ref, acc_ref):
    cols = [s[:, c * LANES:(c + 1) * LANES] for c in range(s.shape[1] // LANES)]
    m_prev = m_ref[...]
    m_next = jnp.maximum(m_prev, jnp.max(functools.reduce(jnp.maximum, cols), axis=1, keepdims=True))
    alpha = jnp.exp2(m_prev - m_next)
    ps = [jnp.exp2(c - m_next) for c in cols]
    l_ref[...] = alpha * l_ref[...] + functools.reduce(jnp.add, ps)
    p = jnp.concatenate([c.astype(BF16) for c in ps], axis=1)
    acc_ref[...] = alpha * acc_ref[...] + _dot(p, v)
    m_ref[...] = m_next


ONES_ROWS = 16


def _softmax_update_t(s, cmax, vt, m_ref, acc_ref):
    m_prev = m_ref[...]
    m_next = jnp.maximum(m_prev, cmax)
    alpha = jnp.exp2(m_prev - m_next)
    p = jnp.exp2(s - m_next).astype(BF16)
    vt1 = jnp.concatenate([vt, jnp.ones((ONES_ROWS, vt.shape[1]), BF16)], axis=0)
    acc_ref[...] = alpha * acc_ref[...] + _dot(vt1, p)
    m_ref[...] = m_next


SLOT_A, SLOT_B, SLOT_NEXT = 0, 1, 2


def _attn_prompt_kernel(q_ref, qn_ref, k_ref, vt_ref, sub_ref, lq1_ref, lk1_ref, lq2_ref, lk2_ref, o_ref,
                        s_sc, cm_sc, m_sc, acc_sc):
    t = q_ref.shape[1]
    i = pl.program_id(2)
    qs = _stack_q(q_ref[0])
    m_sc[...] = jnp.full(m_sc.shape, NEG, F32)
    acc_sc[...] = jnp.zeros(acc_sc.shape, F32)

    def store_scores(s, slot):
        s_sc[slot] = s
        cm_sc[slot] = jnp.max(s, axis=0, keepdims=True)

    def S(j, slot, masked=False):
        kj = k_ref[0, pl.ds(pl.multiple_of(j * t, t), t), :]
        s = _dot_nt(kj, qs)
        if masked:
            key = lax.broadcasted_iota(jnp.int32, s.shape, 0)
            qry = lax.broadcasted_iota(jnp.int32, s.shape, 1)
            s = jnp.where(_chunk_of(key) <= _chunk_of(qry & (t - 1)), s, NEG)
        store_scores(s, slot)

    def N():
        store_scores(_dot_nt(k_ref[0, 0:t, :], _stack_q(qn_ref[0])), SLOT_NEXT)

    def C(j, slot):
        vtj = vt_ref[0, :, pl.ds(pl.multiple_of(j * t, t), t)]
        _softmax_update_t(s_sc[slot], cm_sc[slot], vtj, m_sc, acc_sc)

    @pl.when(i == 0)
    def _():
        S(0, SLOT_A, True)
        N()
        C(0, SLOT_A)

    @pl.when(i == 1)
    def _():
        S(1, SLOT_A, True)
        C(0, SLOT_NEXT)
        N()
        C(1, SLOT_A)

    @pl.when(i >= 2)
    def _():
        S(1, SLOT_A)
        C(0, SLOT_NEXT)

        def pair(j):
            S(j + 1, SLOT_B)
            C(j, SLOT_A)
            S(j + 2, SLOT_A)
            C(j + 1, SLOT_B)

        def quad(jj, c):
            pair(1 + 4 * jj)
            pair(3 + 4 * jj)
            return c

        def single(jj, c):
            pair(1 + 2 * jj)
            return c

        npairs = (i - 2) // 2
        lax.fori_loop(0, npairs // 2, quad, 0)
        lax.fori_loop(npairs - npairs % 2, npairs, single, 0)

        @pl.when(i % 2 == 0)
        def _():
            S(i, SLOT_B, True)
            C(i - 1, SLOT_A)
            N()
            C(i, SLOT_B)

        @pl.when(i % 2 == 1)
        def _():
            S(i - 1, SLOT_B)
            C(i - 2, SLOT_A)
            S(i, SLOT_A, True)
            C(i - 1, SLOT_B)
            N()
            C(i, SLOT_A)

    lam = _lambda(lq1_ref[...], lk1_ref[...], lq2_ref[...], lk2_ref[...])
    on = acc_sc[0:V_DIM, :] * (1.0 / acc_sc[V_DIM:V_DIM + 1, :])
    o = on[:, :t] - lam * on[:, t:]
    inv = lax.rsqrt(jnp.mean(o * o, axis=0, keepdims=True) + EPS)
    o_ref[0] = (o * inv * (sub_ref[...] * (1.0 - LAMBDA_INIT))).astype(BF16)


def _attn_prompt(q, k, vt, subln, lq1, lk1, lq2, lk2, *, t):
    b, s, _ = q.shape
    nq = s // t
    small = lambda a: _resident(a.shape)
    subln = subln.reshape(V_DIM, 1)
    return pl.pallas_call(
        _attn_prompt_kernel,
        grid=(b, N_HEADS, nq),
        in_specs=[
            pl.BlockSpec((1, t, V_DIM), lambda bi, h, i: (bi, i, h)),
            pl.BlockSpec((1, t, V_DIM), lambda bi, h, i: (bi, jnp.minimum(i + 1, nq - 1), h)),
            pl.BlockSpec((1, s, V_DIM), lambda bi, h, i: (bi, 0, h)),
            pl.BlockSpec((1, V_DIM, s), lambda bi, h, i: (bi, h, 0)),
            small(subln), small(lq1), small(lk1), small(lq2), small(lk2),
        ],
        out_specs=pl.BlockSpec((1, V_DIM, t), lambda bi, h, i: (bi, h, i)),
        out_shape=jax.ShapeDtypeStruct((b, D_MODEL, s), BF16),
        scratch_shapes=[pltpu.VMEM((3, t, 2 * t), F32), pltpu.VMEM((3, 1, 2 * t), F32),
                        pltpu.VMEM((1, 2 * t), F32), pltpu.VMEM((V_DIM + ONES_ROWS, 2 * t), F32)],
        compiler_params=pltpu.CompilerParams(dimension_semantics=("arbitrary",) * 3, vmem_limit_bytes=VMEM_LIMIT),
        name="attn_prompt",
    )(q, q, k, vt, subln, lq1, lk1, lq2, lk2)


def _attn_sample_kernel(q_ref, kc_ref, vc_ref, kn_ref, vn_ref, sub_ref, lq1_ref, lk1_ref, lq2_ref, lk2_ref,
                        o_ref, m_sc, l_sc, acc_sc, *, q_start):
    t = q_ref.shape[1]
    tkb = kc_ref.shape[1] // N_HEADS
    j = pl.program_id(1)

    @pl.when(j == 0)
    def _():
        _softmax_init(m_sc, l_sc, acc_sc)

    head_cols = lambda h: slice(h * V_DIM, (h + 1) * V_DIM)
    for h in range(N_HEADS):
        qs = _stack_q(q_ref[0, :, head_cols(h)])
        kh = kc_ref[0, pl.ds(h, tkb, stride=N_HEADS), :].astype(BF16)
        vh = vc_ref[0, pl.ds(h, tkb, stride=N_HEADS), :].astype(BF16)
        _softmax_update(_dot_nt(qs, kh), vh, m_sc.at[h], l_sc.at[h], acc_sc.at[h])

    @pl.when(j == pl.num_programs(1) - 1)
    def _():
        lam = _lambda(lq1_ref[...], lk1_ref[...], lq2_ref[...], lk2_ref[...])
        for h in range(N_HEADS):
            qs = _stack_q(q_ref[0, :, head_cols(h)])
            s_n = _dot_nt(qs, kn_ref[0, :, head_cols(h)])
            row = lax.broadcasted_iota(jnp.int32, s_n.shape, 0)
            col = lax.broadcasted_iota(jnp.int32, s_n.shape, 1)
            s_n = jnp.where(_chunk_of(q_start + col) <= _chunk_of(q_start + (row & (t - 1))), s_n, NEG)
            m_prev = m_sc[h]
            m_fin = jnp.maximum(m_prev, jnp.max(s_n, axis=1, keepdims=True))
            alpha = jnp.exp2(m_prev - m_fin)
            p_n = jnp.exp2(s_n - m_fin[:, :t])
            l = jnp.sum(alpha * l_sc[h], axis=1, keepdims=True) + jnp.sum(p_n, axis=1, keepdims=True)
            acc = alpha * acc_sc[h] + _dot(p_n.astype(BF16), vn_ref[0, :, head_cols(h)])
            o_ref[0, :, head_cols(h)] = _finish(acc, l, t, lam, sub_ref[...])


def _attn_sample(q, cache_k, cache_v, kn, vn, subln, lq1, lk1, lq2, lk2, *, tkb):
    b, t, _ = q.shape
    past = cache_k.shape[1]
    kc = cache_k.reshape(b, past * N_HEADS, V_DIM)
    vc = cache_v.reshape(b, past * N_HEADS, V_DIM)
    small = lambda a: _resident(a.shape)
    rows = pl.BlockSpec((1, t, D_MODEL), lambda bi, j: (bi, 0, 0))
    blk = pl.BlockSpec((1, tkb * N_HEADS, V_DIM), lambda bi, j: (bi, j, 0))
    return pl.pallas_call(
        functools.partial(_attn_sample_kernel, q_start=past),
        grid=(b, past // tkb),
        in_specs=[rows, blk, blk, rows, rows, small(subln), small(lq1), small(lk1), small(lq2), small(lk2)],
        out_specs=rows,
        out_shape=jax.ShapeDtypeStruct((b, t, D_MODEL), BF16),
        scratch_shapes=[pltpu.VMEM((N_HEADS, 2 * t, LANES), F32), pltpu.VMEM((N_HEADS, 2 * t, LANES), F32),
                        pltpu.VMEM((N_HEADS, 2 * t, V_DIM), F32)],
        compiler_params=pltpu.CompilerParams(dimension_semantics=("arbitrary",) * 2, vmem_limit_bytes=VMEM_LIMIT),
        name="attn_sample",
    )(q, kc, vc, kn, vn, subln, lq1, lk1, lq2, lk2)


def _trunks(x_p, x_s, conv_p0, conv_s0, cache_k, cache_v, w, *, tm_conv, t_attn, tkb):
    b, s, _ = x_p.shape
    db, ds, _ = x_s.shape
    rows = lambda a: a.reshape(-1, D_MODEL)
    seqs = lambda a, like: a.reshape(like.shape[0], like.shape[1], D_MODEL)
    ffn = functools.partial(_ffn, w_in=w["ffn_in"], w_out=w["ffn_out"], tm=db * ds, seq=s)

    h_p, h_s, _, _ = ffn(rows(x_p), rows(x_s), w["g_ffn1"][0], layer=(0, 0))
    conv_w = (w["g_mix"][0], w["conv_in"], w["conv_w"], w["conv_out"])
    h_p, conv_p = _conv(seqs(h_p, x_p), conv_p0, *conv_w, tm=tm_conv)
    h_s, conv_s = _conv_streams(seqs(h_s, x_s), conv_s0, *conv_w)
    kv = dict(layer=(0, 1), proj="kv", proj_gain=w["g_kv"], proj_w=w["w_kv"])
    x1_p, _, (k_p, v_p, kb_p, vt_p), _ = ffn(rows(h_p), None, w["g_ffn2"][0], v_cols=True, **kv)
    x1_s, _, (k_s, v_s, kb_s, vb_s), _ = ffn(rows(h_s), None, w["g_ffn2"][0], **kv)
    h1_p, h1_s, (q_p,), (q_s,) = ffn(x1_p, x1_s, w["g_ffn1"][1], layer=(1, 0), proj="q",
                                     proj_gain=w["g_mix"][1], proj_w=w["w_q"])
    lam = (w["lq1"], w["lk1"], w["lq2"], w["lk2"])
    o_p = _attn_prompt(seqs(q_p, x_p), seqs(kb_p, x_p), vt_p, w["subln"], *lam, t=t_attn)
    o_s = _attn_sample(seqs(q_s, x_s), cache_k, cache_v, seqs(kb_s, x_s), seqs(vb_s, x_s), w["subln"], *lam, tkb=tkb)
    y_p, y_s, _, _ = ffn(h1_p, h1_s, w["g_ffn2"][1], layer=(1, 1), pre=(o_p, rows(o_s), w["w_o"]),
                         final_gain=w["g_final"])
    heads = lambda a, like: a.reshape(like.shape[0], like.shape[1], N_HEADS, V_DIM)
    return (seqs(y_p, x_p), seqs(y_s, x_s), conv_p, heads(k_p, x_p), heads(v_p, x_p),
            conv_s, heads(k_s, x_s), heads(v_s, x_s))


def kernel(x_prompt, x_sample, state_conv, cache_k, cache_v, norm_ffn1, norm_mix, norm_ffn2, ffn_w_in, ffn_w_out,
           conv_w_in, conv_w, conv_w_out, norm_kv, w_k, w_v, w_q, lambda_q1, lambda_k1, lambda_q2, lambda_k2,
           subln, w_o, norm_final):
    depth = norm_ffn1.shape[0]
    row = lambda a: a.reshape(1, -1)
    w = {
        "g_ffn1": [row(norm_ffn1[i]) for i in range(depth)],
        "g_mix": [row(norm_mix[i]) for i in range(depth)],
        "g_ffn2": [row(norm_ffn2[i]) for i in range(depth)],
        "ffn_in": ffn_w_in.astype(BF16),
        "ffn_out": ffn_w_out.astype(BF16),
        "conv_in": conv_w_in[0].astype(BF16),
        "conv_w": conv_w[0],
        "conv_out": conv_w_out[0].astype(BF16),
        "g_kv": row(norm_kv),
        "w_kv": jnp.concatenate([w_k, w_v], axis=1).astype(BF16),
        "w_q": w_q[0].astype(BF16),
        "lq1": row(lambda_q1[0]), "lk1": row(lambda_k1[0]), "lq2": row(lambda_q2[0]), "lk2": row(lambda_k2[0]),
        "subln": row(subln[0]),
        "w_o": w_o[0].astype(BF16),
        "g_final": row(norm_final),
    }
    b = x_prompt.shape[0]
    conv0 = jnp.zeros((b, 1, CONV_W - 1, D_MODEL), x_prompt.dtype)
    return _trunks(x_prompt, x_sample, conv0, state_conv, cache_k, cache_v, w, tm_conv=512, t_attn=512,
                   tkb=min(2048, cache_k.shape[1]))
```

```python
import functools
import math

import jax
import jax.numpy as jnp
from jax import lax
from jax.experimental import pallas as pl
from jax.experimental.pallas import tpu as pltpu

D_MODEL = 1024
CHUNK = 64
CONV_W = 3
D_FF = 2816
N_HEADS = 8
HEAD_DIM = 64
V_DIM = 2 * HEAD_DIM
EPS = 1e-6
NEG = -1e30
LAMBDA_INIT = 0.8 - 0.6 * math.exp(-0.3 * 1)
LOG2E = math.log2(math.e)
Q_SCALE = HEAD_DIM ** -0.5 * LOG2E

V7X_VMEM_BYTES = 64 * 1024 * 1024
VMEM_LIMIT = V7X_VMEM_BYTES - 8 * 1024 * 1024
LANES = 128

BF16 = jnp.bfloat16
F32 = jnp.float32


def _resident(shape):
    nd = len(shape)
    return pl.BlockSpec(shape, lambda *_: (0,) * nd, pipeline_mode=pl.Buffered(1))


def _resident_slice(shape, lead):
    tail = tuple(shape[len(lead):])
    index = tuple(lead) + (0,) * len(tail)
    return pl.BlockSpec((None,) * len(lead) + tail, lambda *_: index, pipeline_mode=pl.Buffered(1))


def _rms(x, g):
    return x * lax.rsqrt(jnp.mean(x * x, axis=-1, keepdims=True) + EPS) * g


def _dot(a, b):
    return jnp.dot(a, b, preferred_element_type=F32)


def _dot_nt(a, b):
    return lax.dot_general(a, b, (((1,), (1,)), ((), ())), preferred_element_type=F32)


def _ffn_kernel(*refs, pre_proj, final_norm, proj):
    refs = list(refs)
    x_ref = refs.pop(0)
    if pre_proj:
        a_ref, wpre_ref = refs.pop(0), refs.pop(0)
    g_ref, win_ref, wout_ref = refs.pop(0), refs.pop(0), refs.pop(0)
    if final_norm:
        gf_ref = refs.pop(0)
    if proj is not None:
        gp_ref, wp_ref = refs.pop(0), refs.pop(0)
    y_ref = refs.pop(0)

    x = x_ref[...]
    if pre_proj == "rows":
        x = x + _dot(a_ref[...], wpre_ref[...])
    elif pre_proj == "cols":
        x = x + lax.dot_general(a_ref[0], wpre_ref[...], (((0,), (0,)), ((), ())), preferred_element_type=F32)
    xn = _rms(x, g_ref[...]).astype(BF16)
    gu = _dot(xn, win_ref[...])
    g, u = gu[:, :D_FF], gu[:, D_FF:]
    act = (g * (1.0 / (1.0 + jnp.exp(-g))) * u).astype(BF16)
    y = x + 0.5 * _dot(act, wout_ref[...])
    y_ref[...] = _rms(y, gf_ref[...]) if final_norm else y

    if proj in ("kv", "kvt"):
        kvn = _rms(y, gp_ref[...]).astype(BF16)
        r = _dot(kvn, wp_ref[...])
        k, v = r[:, :D_MODEL], r[:, D_MODEL:]
        if proj == "kv":
            kf_ref, vf_ref, kb_ref, vb_ref = refs
            vb_ref[...] = v.astype(BF16)
        else:
            kf_ref, vf_ref, kb_ref, vtb_ref = refs
            vtb_ref[0] = v.T.astype(BF16)
        kf_ref[...] = k
        vf_ref[...] = v
        kb_ref[...] = k.astype(BF16)
    elif proj == "q":
        (qb_ref,) = refs
        r = _dot(_rms(y, gp_ref[...]).astype(BF16), wp_ref[...])
        qb_ref[...] = (r * Q_SCALE).astype(BF16)


def _ffn(x, gain, w_in, w_out, layer, *, tm, pre=None, final_gain=None, proj=None, proj_gain=None, proj_w=None,
         seq=None):
    n = x.shape[0]
    tm = min(tm, n)
    row = lambda cols: pl.BlockSpec((tm, cols), lambda i: (i, 0))
    args, specs = [x], [row(D_MODEL)]
    pre_kind = None
    if pre is not None:
        a, w_pre = pre
        args += [a, w_pre]
        if a.ndim == 2:
            pre_kind = "rows"
            specs += [row(D_MODEL), _resident(w_pre.shape)]
        else:
            pre_kind = "cols"
            nsb_a = a.shape[2] // tm
            specs += [pl.BlockSpec((1, D_MODEL, tm), lambda i: (i // nsb_a, 0, i % nsb_a)), _resident(w_pre.shape)]
    args += [gain, w_in, w_out]
    specs += [_resident(gain.shape), _resident_slice(w_in.shape, layer), _resident_slice(w_out.shape, layer)]
    if final_gain is not None:
        args.append(final_gain)
        specs.append(_resident(final_gain.shape))
    out_shape = [jax.ShapeDtypeStruct((n, D_MODEL), F32)]
    out_specs = [row(D_MODEL)]
    if proj is not None:
        args += [proj_gain, proj_w]
        specs += [_resident(proj_gain.shape), _resident(proj_w.shape)]
        dts = {"kv": (F32, F32, BF16, BF16), "kvt": (F32, F32, BF16), "q": (BF16,)}[proj]
        out_shape += [jax.ShapeDtypeStruct((n, D_MODEL), dt) for dt in dts]
        out_specs += [row(D_MODEL) for _ in dts]
        if proj == "kvt":
            nsb = seq // tm
            out_shape.append(jax.ShapeDtypeStruct((n // seq, D_MODEL, seq), BF16))
            out_specs.append(pl.BlockSpec((1, D_MODEL, tm), lambda i: (i // nsb, 0, i % nsb)))
    return pl.pallas_call(
        functools.partial(_ffn_kernel, pre_proj=pre_kind, final_norm=final_gain is not None, proj=proj),
        grid=(n // tm,),
        in_specs=specs,
        out_specs=out_specs,
        out_shape=out_shape,
        compiler_params=pltpu.CompilerParams(dimension_semantics=("arbitrary",), vmem_limit_bytes=VMEM_LIMIT),
        name="ffn" + ("_pre" if pre is not None else "") + ("_" + proj if proj else "") + ("_fin" if final_gain is not None else ""),
    )(*args)


def _conv_kernel(x_ref, st_ref, g_ref, win_ref, cw_ref, wout_ref, o_ref, ns_ref, carry):
    tm = x_ref.shape[1]

    @pl.when(pl.program_id(1) == 0)
    def _():
        carry[0:CONV_W - 1, :] = st_ref[0, 0]

    h = x_ref[0]
    hn = _rms(h, g_ref[...]).astype(BF16)
    proj = _dot(hn, win_ref[...])
    b_g, c_g, hh = proj[:, :D_MODEL], proj[:, D_MODEL:2 * D_MODEL], proj[:, 2 * D_MODEL:]
    u = c_g * hh
    prev0, prev1 = carry[0:1, :], carry[1:2, :]
    row = lax.broadcasted_iota(jnp.int32, (tm, D_MODEL), 0)
    r1 = pltpu.roll(u, 1, 0)
    r2 = pltpu.roll(u, 2, 0)
    u1 = jnp.where(row == 0, prev1, r1)
    u2 = jnp.where(row == 0, prev0, jnp.where(row == 1, prev1, r2))
    cw = cw_ref[...]
    conv = cw[0:1, :] * u2 + cw[1:2, :] * u1 + cw[2:3, :] * u
    o_ref[0] = h + _dot((b_g * conv).astype(BF16), wout_ref[...])
    tail = r2[0:CONV_W - 1, :]
    carry[0:CONV_W - 1, :] = tail
    ns_ref[0, 0] = tail


def _conv_streams_kernel(x_ref, st_ref, g_ref, win_ref, cw_ref, wout_ref, o_ref, ns_ref, *, s):
    n = x_ref.shape[0]
    b = n // s
    h = x_ref[...]
    hn = _rms(h, g_ref[...]).astype(BF16)
    proj = _dot(hn, win_ref[...])
    b_g, c_g, hh = proj[:, :D_MODEL], proj[:, D_MODEL:2 * D_MODEL], proj[:, 2 * D_MODEL:]
    u = c_g * hh
    st = st_ref[...]
    per_row = lambda rows: jnp.broadcast_to(rows, (b, s, D_MODEL)).reshape(n, D_MODEL)
    prev0, prev1 = per_row(st[:, 0:1, :]), per_row(st[:, 1:2, :])
    pos = lax.broadcasted_iota(jnp.int32, (n, D_MODEL), 0) & (s - 1)
    u1 = jnp.where(pos == 0, prev1, pltpu.roll(u, 1, 0))
    u2 = jnp.where(pos == 0, prev0, jnp.where(pos == 1, prev1, pltpu.roll(u, 2, 0)))
    cw = cw_ref[...]
    conv = cw[0:1, :] * u2 + cw[1:2, :] * u1 + cw[2:3, :] * u
    o_ref[...] = h + _dot((b_g * conv).astype(BF16), wout_ref[...])
    ns_ref[...] = pltpu.roll(u, n - (s - (CONV_W - 1)), 0).reshape(b, s, D_MODEL)[:, 0:CONV_W - 1, :]


def _conv_streams(x, state, gain, w_in, conv_w, w_out):
    b, s, _ = x.shape
    assert s & (s - 1) == 0 and s % 8 == 0, s
    n = b * s
    full = lambda shape: pl.BlockSpec(shape, lambda i: (0,) * len(shape))
    y, ns = pl.pallas_call(
        functools.partial(_conv_streams_kernel, s=s),
        grid=(1,),
        in_specs=[full((n, D_MODEL)), full((b, CONV_W - 1, D_MODEL)),
                  full(gain.shape), full(w_in.shape), full(conv_w.shape), full(w_out.shape)],
        out_specs=[full((n, D_MODEL)), full((b, CONV_W - 1, D_MODEL))],
        out_shape=[jax.ShapeDtypeStruct((n, D_MODEL), F32), jax.ShapeDtypeStruct((b, CONV_W - 1, D_MODEL), F32)],
        compiler_params=pltpu.CompilerParams(dimension_semantics=("arbitrary",), vmem_limit_bytes=VMEM_LIMIT),
        name="conv_streams",
    )(x.reshape(n, D_MODEL), state.reshape(b, CONV_W - 1, D_MODEL), gain, w_in, conv_w, w_out)
    return y.reshape(b, s, D_MODEL), ns.reshape(b, 1, CONV_W - 1, D_MODEL)


def _conv(x, state, gain, w_in, conv_w, w_out, *, tm):
    b, s, _ = x.shape
    tm = min(tm, s)
    return pl.pallas_call(
        _conv_kernel,
        grid=(b, s // tm),
        in_specs=[
            pl.BlockSpec((1, tm, D_MODEL), lambda bi, i: (bi, i, 0)),
            pl.BlockSpec((1, 1, CONV_W - 1, D_MODEL), lambda bi, i: (bi, 0, 0, 0)),
            _resident(gain.shape), _resident(w_in.shape), _resident(conv_w.shape), _resident(w_out.shape),
        ],
        out_specs=[
            pl.BlockSpec((1, tm, D_MODEL), lambda bi, i: (bi, i, 0)),
            pl.BlockSpec((1, 1, CONV_W - 1, D_MODEL), lambda bi, i: (bi, 0, 0, 0)),
        ],
        out_shape=[jax.ShapeDtypeStruct((b, s, D_MODEL), F32),
                   jax.ShapeDtypeStruct((b, 1, CONV_W - 1, D_MODEL), F32)],
        scratch_shapes=[pltpu.VMEM((8, D_MODEL), F32)],
        compiler_params=pltpu.CompilerParams(dimension_semantics=("arbitrary", "arbitrary"),
                                             vmem_limit_bytes=VMEM_LIMIT),
        name="conv",
    )(x, state, gain, w_in, conv_w, w_out)


def _stack_q(q):
    lane = lax.broadcasted_iota(jnp.int32, q.shape, 1)
    zero = jnp.zeros_like(q)
    return jnp.concatenate([jnp.where(lane < HEAD_DIM, q, zero), jnp.where(lane >= HEAD_DIM, q, zero)], axis=0)


def _chunk_of(pos):
    return lax.shift_right_logical(pos, CHUNK.bit_length() - 1)


def _lambda(lq1, lk1, lq2, lk2):
    return (jnp.exp(jnp.sum(lq1 * lk1, axis=-1, keepdims=True))
            - jnp.exp(jnp.sum(lq2 * lk2, axis=-1, keepdims=True)) + LAMBDA_INIT)


def _finish(acc, l, t, lam, subln):
    o = acc[:t] / l[:t] - lam * (acc[t:] / l[t:])
    return (_rms(o, subln) * (1.0 - LAMBDA_INIT)).astype(BF16)


def _softmax_init(m_ref, l_ref, acc_ref):
    m_ref[...] = jnp.full(m_ref.shape, NEG, F32)
    l_ref[...] = jnp.zeros(l_ref.shape, F32)
    acc_ref[...] = jnp.zeros(acc_ref.shape, F32)


def _softmax_update(s, v, m_ref, l_ref, acc_ref):
    cols = [s[:, c * LANES:(c + 1) * LANES] for c in range(s.shape[1] // LANES)]
    m_prev = m_ref[...]
    m_next = jnp.maximum(m_prev, jnp.max(functools.reduce(jnp.maximum, cols), axis=1, keepdims=True))
    alpha = jnp.exp2(m_prev - m_next)
    ps = [jnp.exp2(c - m_next) for c in cols]
    l_ref[...] = alpha * l_ref[...] + functools.reduce(jnp.add, ps)
    p = jnp.concatenate([c.astype(BF16) for c in ps], axis=1)
    acc_ref[...] = alpha * acc_ref[...] + _dot(p, v)
    m_ref[...] = m_next


ONES_ROWS = 16


def _softmax_update_t(s, cmax, vt, m_ref, acc_ref):
    m_prev = m_ref[...]
    m_next = jnp.maximum(m_prev, cmax)
    alpha = jnp.exp2(m_prev - m_next)
    p = jnp.exp2(s - m_next).astype(BF16)
    vt1 = jnp.concatenate([vt, jnp.ones((ONES_ROWS, vt.shape[1]), BF16)], axis=0)
    acc_ref[...] = alpha * acc_ref[...] + _dot(vt1, p)
    m_ref[...] = m_next


SLOT_A, SLOT_B = 0, 1


def _attn_prompt_kernel(fi_ref, fj_ref, q_ref, k_ref, vt_ref, sub_ref, lq1_ref, lk1_ref, lq2_ref, lk2_ref, o_ref,
                        qs_sc, s_sc, cm_sc, m_sc, acc_sc, *, t):
    nq = q_ref.shape[1] // t
    nfull = nq * (nq - 1) // 2
    for i in range(nq):
        qs_sc[i] = _stack_q(q_ref[0, i * t:(i + 1) * t, :])
    m_sc[...] = jnp.full(m_sc.shape, NEG, F32)
    acc_sc[...] = jnp.zeros(acc_sc.shape, F32)

    def S(i, j, slot, masked):
        kj = k_ref[0, pl.ds(pl.multiple_of(j * t, t), t), :]
        s = _dot_nt(kj, qs_sc[i])
        if masked:
            key = lax.broadcasted_iota(jnp.int32, s.shape, 0)
            qry = lax.broadcasted_iota(jnp.int32, s.shape, 1)
            s = jnp.where(_chunk_of(key) <= _chunk_of(qry & (t - 1)), s, NEG)
        s_sc[slot] = s
        cm_sc[slot] = jnp.max(s, axis=0, keepdims=True)

    def C(i, j, slot):
        vtj = vt_ref[0, :, pl.ds(pl.multiple_of(j * t, t), t)]
        _softmax_update_t(s_sc[slot], cm_sc[slot], vtj, m_sc.at[i], acc_sc.at[i])

    s_full = lambda n, slot: S(fi_ref[n], fj_ref[n], slot, False)
    c_full = lambda n, slot: C(fi_ref[n], fj_ref[n], slot)
    s_diag = lambda d, slot: S(d, d, slot, True)
    c_diag = lambda d, slot: C(d, d, slot)

    if nfull:
        s_full(0, SLOT_A)
    else:
        s_diag(0, SLOT_A)
    nquads = max(nfull - 1, 0) // 4

    def quad(n4, c):
        n = 4 * n4
        s_full(n + 1, SLOT_B)
        c_full(n, SLOT_A)
        s_full(n + 2, SLOT_A)
        c_full(n + 1, SLOT_B)
        s_full(n + 3, SLOT_B)
        c_full(n + 2, SLOT_A)
        s_full(n + 4, SLOT_A)
        c_full(n + 3, SLOT_B)
        return c

    lax.fori_loop(0, nquads, quad, 0)
    slots = (SLOT_A, SLOT_B)
    for r, n in enumerate(range(4 * nquads, nfull)):
        cur, nxt = slots[r % 2], slots[(r + 1) % 2]
        if n + 1 < nfull:
            s_full(n + 1, nxt)
        else:
            s_diag(0, nxt)
        c_full(n, cur)
    first, second = slots[(nfull - 4 * nquads) % 2], slots[(nfull - 4 * nquads + 1) % 2]

    npairs = (nq - 1) // 2

    def pair(d2, c):
        d = 2 * d2
        s_diag(d + 1, second)
        c_diag(d, first)
        s_diag(d + 2, first)
        c_diag(d + 1, second)
        return c

    lax.fori_loop(0, npairs, pair, 0)
    if nq - 2 * npairs == 2:
        s_diag(nq - 1, second)
        c_diag(nq - 2, first)
        c_diag(nq - 1, second)
    else:
        c_diag(nq - 1, first)

    lam = _lambda(lq1_ref[...], lk1_ref[...], lq2_ref[...], lk2_ref[...])
    gain = sub_ref[...] * (1.0 - LAMBDA_INIT)

    def finish(i, c):
        acc = acc_sc[i]
        on = acc[0:V_DIM, :] * (1.0 / acc[V_DIM:V_DIM + 1, :])
        o = on[:, :t] - lam * on[:, t:]
        inv = lax.rsqrt(jnp.mean(o * o, axis=0, keepdims=True) + EPS)
        o_ref[0, :, pl.ds(pl.multiple_of(i * t, t), t)] = (o * inv * gain).astype(BF16)
        return c

    lax.fori_loop(0, nq, finish, 0)


def _attn_prompt(q, k, vt, subln, lq1, lk1, lq2, lk2, *, t):
    b, s, _ = q.shape
    nq = s // t
    pairs = [(i, j) for i in range(1, nq) for j in range(i)] or [(0, 0)]
    fi = jnp.asarray([p[0] for p in pairs], jnp.int32)
    fj = jnp.asarray([p[1] for p in pairs], jnp.int32)
    small = lambda a: pl.BlockSpec(a.shape, lambda bi, h, *_: (0,) * a.ndim, pipeline_mode=pl.Buffered(1))
    subln = subln.reshape(V_DIM, 1)
    rows = pl.BlockSpec((1, s, V_DIM), lambda bi, h, *_: (bi, 0, h))
    cols = pl.BlockSpec((1, V_DIM, s), lambda bi, h, *_: (bi, h, 0))
    return pl.pallas_call(
        functools.partial(_attn_prompt_kernel, t=t),
        grid_spec=pltpu.PrefetchScalarGridSpec(
            num_scalar_prefetch=2,
            grid=(b, N_HEADS),
            in_specs=[rows, rows, cols, small(subln), small(lq1), small(lk1), small(lq2), small(lk2)],
            out_specs=cols,
            scratch_shapes=[pltpu.VMEM((nq, 2 * t, V_DIM), BF16), pltpu.VMEM((2, t, 2 * t), F32),
                            pltpu.VMEM((2, 1, 2 * t), F32), pltpu.VMEM((nq, 1, 2 * t), F32),
                            pltpu.VMEM((nq, V_DIM + ONES_ROWS, 2 * t), F32)],
        ),
        out_shape=jax.ShapeDtypeStruct((b, D_MODEL, s), BF16),
        compiler_params=pltpu.CompilerParams(dimension_semantics=("arbitrary",) * 2, vmem_limit_bytes=VMEM_LIMIT),
        name="attn_prompt",
    )(fi, fj, q, k, vt, subln, lq1, lk1, lq2, lk2)


def _attn_sample_kernel(q_ref, kc_ref, vc_ref, kn_ref, vn_ref, sub_ref, lq1_ref, lk1_ref, lq2_ref, lk2_ref,
                        o_ref, m_sc, l_sc, acc_sc, *, q_start):
    t = q_ref.shape[1]
    tkb = kc_ref.shape[1] // N_HEADS
    j = pl.program_id(1)

    @pl.when(j == 0)
    def _():
        _softmax_init(m_sc, l_sc, acc_sc)

    head_cols = lambda h: slice(h * V_DIM, (h + 1) * V_DIM)
    for h in range(N_HEADS):
        qs = _stack_q(q_ref[0, :, head_cols(h)])
        kh = kc_ref[0, pl.ds(h, tkb, stride=N_HEADS), :].astype(BF16)
        vh = vc_ref[0, pl.ds(h, tkb, stride=N_HEADS), :].astype(BF16)
        _softmax_update(_dot_nt(qs, kh), vh, m_sc.at[h], l_sc.at[h], acc_sc.at[h])

    @pl.when(j == pl.num_programs(1) - 1)
    def _():
        lam = _lambda(lq1_ref[...], lk1_ref[...], lq2_ref[...], lk2_ref[...])
        for h in range(N_HEADS):
            qs = _stack_q(q_ref[0, :, head_cols(h)])
            s_n = _dot_nt(qs, kn_ref[0, :, head_cols(h)])
            row = lax.broadcasted_iota(jnp.int32, s_n.shape, 0)
            col = lax.broadcasted_iota(jnp.int32, s_n.shape, 1)
            s_n = jnp.where(_chunk_of(q_start + col) <= _chunk_of(q_start + (row & (t - 1))), s_n, NEG)
            m_prev = m_sc[h]
            m_fin = jnp.maximum(m_prev, jnp.max(s_n, axis=1, keepdims=True))
            alpha = jnp.exp2(m_prev - m_fin)
            p_n = jnp.exp2(s_n - m_fin[:, :t])
            l = jnp.sum(alpha * l_sc[h], axis=1, keepdims=True) + jnp.sum(p_n, axis=1, keepdims=True)
            acc = alpha * acc_sc[h] + _dot(p_n.astype(BF16), vn_ref[0, :, head_cols(h)])
            o_ref[0, :, head_cols(h)] = _finish(acc, l, t, lam, sub_ref[...])


def _attn_sample(q, cache_k, cache_v, kn, vn, subln, lq1, lk1, lq2, lk2, *, tkb):
    b, t, _ = q.shape
    past = cache_k.shape[1]
    kc = cache_k.reshape(b, past * N_HEADS, V_DIM)
    vc = cache_v.reshape(b, past * N_HEADS, V_DIM)
    small = lambda a: _resident(a.shape)
    rows = pl.BlockSpec((1, t, D_MODEL), lambda bi, j: (bi, 0, 0))
    blk = pl.BlockSpec((1, tkb * N_HEADS, V_DIM), lambda bi, j: (bi, j, 0))
    return pl.pallas_call(
        functools.partial(_attn_sample_kernel, q_start=past),
        grid=(b, past // tkb),
        in_specs=[rows, blk, blk, rows, rows, small(subln), small(lq1), small(lk1), small(lq2), small(lk2)],
        out_specs=rows,
        out_shape=jax.ShapeDtypeStruct((b, t, D_MODEL), BF16),
        scratch_shapes=[pltpu.VMEM((N_HEADS, 2 * t, LANES), F32), pltpu.VMEM((N_HEADS, 2 * t, LANES), F32),
                        pltpu.VMEM((N_HEADS, 2 * t, V_DIM), F32)],
        compiler_params=pltpu.CompilerParams(dimension_semantics=("arbitrary",) * 2, vmem_limit_bytes=VMEM_LIMIT),
        name="attn_sample",
    )(q, kc, vc, kn, vn, subln, lq1, lk1, lq2, lk2)


def _trunk(x, conv_state, cache, w, *, tm_ffn, tm_conv, t_attn):
    b, s, _ = x.shape
    n = b * s
    x2 = x.reshape(n, D_MODEL)
    (h,) = _ffn(x2, w["g_ffn1"][0], w["ffn_in"], w["ffn_out"], (0, 0), tm=tm_ffn)
    conv_args = (h.reshape(b, s, D_MODEL), conv_state, w["g_mix"][0], w["conv_in"], w["conv_w"], w["conv_out"])
    if s < tm_conv:
        h, new_conv = _conv_streams(*conv_args)
    else:
        h, new_conv = _conv(*conv_args, tm=tm_conv)
    x1, k_new, v_new, k_bf, v_bf = _ffn(h.reshape(n, D_MODEL), w["g_ffn2"][0], w["ffn_in"],
                                        w["ffn_out"], (0, 1), tm=tm_ffn, proj="kvt" if cache is None else "kv",
                                        proj_gain=w["g_kv"], proj_w=w["w_kv"], seq=s)
    h1, q_bf = _ffn(x1, w["g_ffn1"][1], w["ffn_in"], w["ffn_out"], (1, 0), tm=tm_ffn, proj="q",
                    proj_gain=w["g_mix"][1], proj_w=w["w_q"])
    shp = (b, s, D_MODEL)
    lam = (w["lq1"], w["lk1"], w["lq2"], w["lk2"])
    if cache is None:
        o = _attn_prompt(q_bf.reshape(shp), k_bf.reshape(shp), v_bf, w["subln"], *lam, t=t_attn)
    else:
        o = _attn_sample(q_bf.reshape(shp), cache[0], cache[1], k_bf.reshape(shp), v_bf.reshape(shp),
                         w["subln"], *lam, tkb=min(2048, cache[0].shape[1])).reshape(n, D_MODEL)
    (y,) = _ffn(h1, w["g_ffn2"][1], w["ffn_in"], w["ffn_out"], (1, 1), tm=tm_ffn,
                pre=(o, w["w_o"]), final_gain=w["g_final"])
    kv_shape = (b, s, N_HEADS, V_DIM)
    return y.reshape(shp), new_conv, k_new.reshape(kv_shape), v_new.reshape(kv_shape)


def kernel(x_prompt, x_sample, state_conv, cache_k, cache_v, norm_ffn1, norm_mix, norm_ffn2, ffn_w_in, ffn_w_out,
           conv_w_in, conv_w, conv_w_out, norm_kv, w_k, w_v, w_q, lambda_q1, lambda_k1, lambda_q2, lambda_k2,
           subln, w_o, norm_final):
    depth = norm_ffn1.shape[0]
    row = lambda a: a.reshape(1, -1)
    w = {
        "g_ffn1": [row(norm_ffn1[i]) for i in range(depth)],
        "g_mix": [row(norm_mix[i]) for i in range(depth)],
        "g_ffn2": [row(norm_ffn2[i]) for i in range(depth)],
        "ffn_in": ffn_w_in.astype(BF16),
        "ffn_out": ffn_w_out.astype(BF16),
        "conv_in": conv_w_in[0].astype(BF16),
        "conv_w": conv_w[0],
        "conv_out": conv_w_out[0].astype(BF16),
        "g_kv": row(norm_kv),
        "w_kv": jnp.concatenate([w_k, w_v], axis=1).astype(BF16),
        "w_q": w_q[0].astype(BF16),
        "lq1": row(lambda_q1[0]), "lk1": row(lambda_k1[0]), "lq2": row(lambda_q2[0]), "lk2": row(lambda_k2[0]),
        "subln": row(subln[0]),
        "w_o": w_o[0].astype(BF16),
        "g_final": row(norm_final),
    }
    b = x_prompt.shape[0]
    conv0 = jnp.zeros((b, 1, CONV_W - 1, D_MODEL), x_prompt.dtype)
    y_p, conv_p, k_p, v_p = _trunk(x_prompt, conv0, None, w, tm_ffn=512, tm_conv=512, t_attn=512)
    y_s, conv_s, k_s, v_s = _trunk(x_sample, state_conv, (cache_k, cache_v), w, tm_ffn=512, tm_conv=512,
                                   t_attn=None)
    return (y_p, y_s, conv_p, k_p, v_p, conv_s, k_s, v_s)
```

```python
import functools
import math

import jax
import jax.numpy as jnp
from jax import lax
from jax.experimental import pallas as pl
from jax.experimental.pallas import tpu as pltpu

D_MODEL = 1024
CHUNK = 64
CONV_W = 3
D_FF = 2816
N_HEADS = 8
HEAD_DIM = 64
V_DIM = 2 * HEAD_DIM
EPS = 1e-6
NEG = -1e30
LAMBDA_INIT = 0.8 - 0.6 * math.exp(-0.3 * 1)
LOG2E = math.log2(math.e)
Q_SCALE = HEAD_DIM ** -0.5 * LOG2E

V7X_VMEM_BYTES = 64 * 1024 * 1024
VMEM_LIMIT = V7X_VMEM_BYTES - 8 * 1024 * 1024
LANES = 128

BF16 = jnp.bfloat16
F32 = jnp.float32


def _resident(shape):
    nd = len(shape)
    return pl.BlockSpec(shape, lambda *_: (0,) * nd, pipeline_mode=pl.Buffered(1))


def _resident_slice(shape, lead):
    tail = tuple(shape[len(lead):])
    index = tuple(lead) + (0,) * len(tail)
    return pl.BlockSpec((None,) * len(lead) + tail, lambda *_: index, pipeline_mode=pl.Buffered(1))


def _rms(x, g):
    return x * lax.rsqrt(jnp.mean(x * x, axis=-1, keepdims=True) + EPS) * g


def _dot(a, b):
    return jnp.dot(a, b, preferred_element_type=F32)


def _dot_nt(a, b):
    return lax.dot_general(a, b, (((1,), (1,)), ((), ())), preferred_element_type=F32)


def _ffn_kernel(*refs, pre_proj, final_norm, proj):
    refs = list(refs)
    x_ref = refs.pop(0)
    if pre_proj:
        a_ref, wpre_ref = refs.pop(0), refs.pop(0)
    g_ref, win_ref, wout_ref = refs.pop(0), refs.pop(0), refs.pop(0)
    if final_norm:
        gf_ref = refs.pop(0)
    if proj is not None:
        gp_ref, wp_ref = refs.pop(0), refs.pop(0)
    y_ref = refs.pop(0)

    x = x_ref[...]
    if pre_proj == "rows":
        x = x + _dot(a_ref[...], wpre_ref[...])
    elif pre_proj == "cols":
        x = x + lax.dot_general(a_ref[0], wpre_ref[...], (((0,), (0,)), ((), ())), preferred_element_type=F32)
    xn = _rms(x, g_ref[...]).astype(BF16)
    gu = _dot(xn, win_ref[...])
    g, u = gu[:, :D_FF], gu[:, D_FF:]
    act = (g * (1.0 / (1.0 + jnp.exp(-g))) * u).astype(BF16)
    y = x + 0.5 * _dot(act, wout_ref[...])
    y_ref[...] = _rms(y, gf_ref[...]) if final_norm else y

    if proj in ("kv", "kvt"):
        kvn = _rms(y, gp_ref[...]).astype(BF16)
        r = _dot(kvn, wp_ref[...])
        k, v = r[:, :D_MODEL], r[:, D_MODEL:]
        if proj == "kv":
            kf_ref, vf_ref, kb_ref, vb_ref = refs
            vb_ref[...] = v.astype(BF16)
        else:
            kf_ref, vf_ref, kb_ref, vtb_ref = refs
            vtb_ref[0] = v.T.astype(BF16)
        kf_ref[...] = k
        vf_ref[...] = v
        kb_ref[...] = k.astype(BF16)
    elif proj in ("q", "qt"):
        (qb_ref,) = refs
        q = _dot(_rms(y, gp_ref[...]).astype(BF16), wp_ref[...]) * Q_SCALE
        if proj == "q":
            qb_ref[...] = q.astype(BF16)
        else:
            qb_ref[0] = q.T.astype(BF16)


def _ffn(x, gain, w_in, w_out, layer, *, tm, pre=None, final_gain=None, proj=None, proj_gain=None, proj_w=None,
         seq=None):
    n = x.shape[0]
    tm = min(tm, n)
    row = lambda cols: pl.BlockSpec((tm, cols), lambda i: (i, 0))
    args, specs = [x], [row(D_MODEL)]
    pre_kind = None
    if pre is not None:
        a, w_pre = pre
        args += [a, w_pre]
        if a.ndim == 2:
            pre_kind = "rows"
            specs += [row(D_MODEL), _resident(w_pre.shape)]
        else:
            pre_kind = "cols"
            nsb_a = a.shape[2] // tm
            specs += [pl.BlockSpec((1, D_MODEL, tm), lambda i: (i // nsb_a, 0, i % nsb_a)), _resident(w_pre.shape)]
    args += [gain, w_in, w_out]
    specs += [_resident(gain.shape), _resident_slice(w_in.shape, layer), _resident_slice(w_out.shape, layer)]
    if final_gain is not None:
        args.append(final_gain)
        specs.append(_resident(final_gain.shape))
    out_shape = [jax.ShapeDtypeStruct((n, D_MODEL), F32)]
    out_specs = [row(D_MODEL)]
    if proj is not None:
        args += [proj_gain, proj_w]
        specs += [_resident(proj_gain.shape), _resident(proj_w.shape)]
        dts = {"kv": (F32, F32, BF16, BF16), "kvt": (F32, F32, BF16), "q": (BF16,), "qt": ()}[proj]
        out_shape += [jax.ShapeDtypeStruct((n, D_MODEL), dt) for dt in dts]
        out_specs += [row(D_MODEL) for _ in dts]
        if proj in ("kvt", "qt"):
            nsb = seq // tm
            out_shape.append(jax.ShapeDtypeStruct((n // seq, D_MODEL, seq), BF16))
            out_specs.append(pl.BlockSpec((1, D_MODEL, tm), lambda i: (i // nsb, 0, i % nsb)))
    return pl.pallas_call(
        functools.partial(_ffn_kernel, pre_proj=pre_kind, final_norm=final_gain is not None, proj=proj),
        grid=(n // tm,),
        in_specs=specs,
        out_specs=out_specs,
        out_shape=out_shape,
        compiler_params=pltpu.CompilerParams(dimension_semantics=("arbitrary",), vmem_limit_bytes=VMEM_LIMIT),
        name="ffn" + ("_pre" if pre is not None else "") + ("_" + proj if proj else "") + ("_fin" if final_gain is not None else ""),
    )(*args)


def _conv_kernel(x_ref, st_ref, g_ref, win_ref, cw_ref, wout_ref, o_ref, ns_ref, carry):
    tm = x_ref.shape[1]

    @pl.when(pl.program_id(1) == 0)
    def _():
        carry[0:CONV_W - 1, :] = st_ref[0, 0]

    h = x_ref[0]
    hn = _rms(h, g_ref[...]).astype(BF16)
    proj = _dot(hn, win_ref[...])
    b_g, c_g, hh = proj[:, :D_MODEL], proj[:, D_MODEL:2 * D_MODEL], proj[:, 2 * D_MODEL:]
    u = c_g * hh
    prev0, prev1 = carry[0:1, :], carry[1:2, :]
    row = lax.broadcasted_iota(jnp.int32, (tm, D_MODEL), 0)
    r1 = pltpu.roll(u, 1, 0)
    r2 = pltpu.roll(u, 2, 0)
    u1 = jnp.where(row == 0, prev1, r1)
    u2 = jnp.where(row == 0, prev0, jnp.where(row == 1, prev1, r2))
    cw = cw_ref[...]
    conv = cw[0:1, :] * u2 + cw[1:2, :] * u1 + cw[2:3, :] * u
    o_ref[0] = h + _dot((b_g * conv).astype(BF16), wout_ref[...])
    tail = r2[0:CONV_W - 1, :]
    carry[0:CONV_W - 1, :] = tail
    ns_ref[0, 0] = tail


def _conv_streams_kernel(x_ref, st_ref, g_ref, win_ref, cw_ref, wout_ref, o_ref, ns_ref, *, s):
    n = x_ref.shape[0]
    b = n // s
    h = x_ref[...]
    hn = _rms(h, g_ref[...]).astype(BF16)
    proj = _dot(hn, win_ref[...])
    b_g, c_g, hh = proj[:, :D_MODEL], proj[:, D_MODEL:2 * D_MODEL], proj[:, 2 * D_MODEL:]
    u = c_g * hh
    st = st_ref[...]
    per_row = lambda rows: jnp.broadcast_to(rows, (b, s, D_MODEL)).reshape(n, D_MODEL)
    prev0, prev1 = per_row(st[:, 0:1, :]), per_row(st[:, 1:2, :])
    pos = lax.broadcasted_iota(jnp.int32, (n, D_MODEL), 0) & (s - 1)
    u1 = jnp.where(pos == 0, prev1, pltpu.roll(u, 1, 0))
    u2 = jnp.where(pos == 0, prev0, jnp.where(pos == 1, prev1, pltpu.roll(u, 2, 0)))
    cw = cw_ref[...]
    conv = cw[0:1, :] * u2 + cw[1:2, :] * u1 + cw[2:3, :] * u
    o_ref[...] = h + _dot((b_g * conv).astype(BF16), wout_ref[...])
    ns_ref[...] = pltpu.roll(u, n - (s - (CONV_W - 1)), 0).reshape(b, s, D_MODEL)[:, 0:CONV_W - 1, :]


def _conv_streams(x, state, gain, w_in, conv_w, w_out):
    b, s, _ = x.shape
    assert s & (s - 1) == 0 and s % 8 == 0, s
    n = b * s
    full = lambda shape: pl.BlockSpec(shape, lambda i: (0,) * len(shape))
    y, ns = pl.pallas_call(
        functools.partial(_conv_streams_kernel, s=s),
        grid=(1,),
        in_specs=[full((n, D_MODEL)), full((b, CONV_W - 1, D_MODEL)),
                  full(gain.shape), full(w_in.shape), full(conv_w.shape), full(w_out.shape)],
        out_specs=[full((n, D_MODEL)), full((b, CONV_W - 1, D_MODEL))],
        out_shape=[jax.ShapeDtypeStruct((n, D_MODEL), F32), jax.ShapeDtypeStruct((b, CONV_W - 1, D_MODEL), F32)],
        compiler_params=pltpu.CompilerParams(dimension_semantics=("arbitrary",), vmem_limit_bytes=VMEM_LIMIT),
        name="conv_streams",
    )(x.reshape(n, D_MODEL), state.reshape(b, CONV_W - 1, D_MODEL), gain, w_in, conv_w, w_out)
    return y.reshape(b, s, D_MODEL), ns.reshape(b, 1, CONV_W - 1, D_MODEL)


def _conv(x, state, gain, w_in, conv_w, w_out, *, tm):
    b, s, _ = x.shape
    tm = min(tm, s)
    return pl.pallas_call(
        _conv_kernel,
        grid=(b, s // tm),
        in_specs=[
            pl.BlockSpec((1, tm, D_MODEL), lambda bi, i: (bi, i, 0)),
            pl.BlockSpec((1, 1, CONV_W - 1, D_MODEL), lambda bi, i: (bi, 0, 0, 0)),
            _resident(gain.shape), _resident(w_in.shape), _resident(conv_w.shape), _resident(w_out.shape),
        ],
        out_specs=[
            pl.BlockSpec((1, tm, D_MODEL), lambda bi, i: (bi, i, 0)),
            pl.BlockSpec((1, 1, CONV_W - 1, D_MODEL), lambda bi, i: (bi, 0, 0, 0)),
        ],
        out_shape=[jax.ShapeDtypeStruct((b, s, D_MODEL), F32),
                   jax.ShapeDtypeStruct((b, 1, CONV_W - 1, D_MODEL), F32)],
        scratch_shapes=[pltpu.VMEM((8, D_MODEL), F32)],
        compiler_params=pltpu.CompilerParams(dimension_semantics=("arbitrary", "arbitrary"),
                                             vmem_limit_bytes=VMEM_LIMIT),
        name="conv",
    )(x, state, gain, w_in, conv_w, w_out)


def _stack_q(q):
    lane = lax.broadcasted_iota(jnp.int32, q.shape, 1)
    zero = jnp.zeros_like(q)
    return jnp.concatenate([jnp.where(lane < HEAD_DIM, q, zero), jnp.where(lane >= HEAD_DIM, q, zero)], axis=0)


def _chunk_of(pos):
    return lax.shift_right_logical(pos, CHUNK.bit_length() - 1)


def _lambda(lq1, lk1, lq2, lk2):
    return (jnp.exp(jnp.sum(lq1 * lk1, axis=-1, keepdims=True))
            - jnp.exp(jnp.sum(lq2 * lk2, axis=-1, keepdims=True)) + LAMBDA_INIT)


def _finish(acc, l, t, lam, subln):
    o = acc[:t] / l[:t] - lam * (acc[t:] / l[t:])
    return (_rms(o, subln) * (1.0 - LAMBDA_INIT)).astype(BF16)


def _softmax_init(m_ref, l_ref, acc_ref):
    m_ref[...] = jnp.full(m_ref.shape, NEG, F32)
    l_ref[...] = jnp.zeros(l_ref.shape, F32)
    acc_ref[...] = jnp.zeros(acc_ref.shape, F32)


def _softmax_update(s, v, m_ref, l_ref, acc_ref):
    cols = [s[:, c * LANES:(c + 1) * LANES] for c in range(s.shape[1] // LANES)]
    m_prev = m_ref[...]
    m_next = jnp.maximum(m_prev, jnp.max(functools.reduce(jnp.maximum, cols), axis=1, keepdims=True))
    alpha = jnp.exp2(m_prev - m_next)
    ps = [jnp.exp2(c - m_next) for c in cols]
    l_ref[...] = alpha * l_ref[...] + functools.reduce(jnp.add, ps)
    p = jnp.concatenate([c.astype(BF16) for c in ps], axis=1)
    acc_ref[...] = alpha * acc_ref[...] + _dot(p, v)
    m_ref[...] = m_next


ONES_ROWS = 16


def _softmax_update_t(s, cmax, vt, m_ref, acc_ref):
    m_prev = m_ref[...]
    m_next = jnp.maximum(m_prev, cmax)
    alpha = jnp.exp2(m_prev - m_next)
    p = jnp.exp2(s - m_next).astype(BF16)
    vt1 = jnp.concatenate([vt, jnp.ones((ONES_ROWS, vt.shape[1]), BF16)], axis=0)
    acc_ref[...] = alpha * acc_ref[...] + _dot(vt1, p)
    m_ref[...] = m_next


SLOT_A, SLOT_B = 0, 1
FULL_UNROLLS = (8, 4)


def _attn_prompt_kernel(fi_ref, fj_ref, qt_ref, k_ref, vt_ref, sub_ref, lq1_ref, lk1_ref, lq2_ref, lk2_ref, o_ref,
                        qs_sc, s_sc, cm_sc, m_sc, acc_sc, *, t):
    nq = qt_ref.shape[2] // t
    nfull = nq * (nq - 1) // 2
    feat = lax.broadcasted_iota(jnp.int32, (V_DIM, t), 0)
    for i in range(nq):
        qt = qt_ref[0, :, i * t:(i + 1) * t]
        zero = jnp.zeros_like(qt)
        qs_sc[i] = jnp.concatenate([jnp.where(feat < HEAD_DIM, qt, zero), jnp.where(feat >= HEAD_DIM, qt, zero)],
                                   axis=1)
    m_sc[...] = jnp.full(m_sc.shape, NEG, F32)
    acc_sc[...] = jnp.zeros(acc_sc.shape, F32)

    def S(i, j, slot, masked):
        kj = k_ref[0, pl.ds(pl.multiple_of(j * t, t), t), :]
        s = _dot(kj, qs_sc[i])
        if masked:
            key = lax.broadcasted_iota(jnp.int32, s.shape, 0)
            qry = lax.broadcasted_iota(jnp.int32, s.shape, 1)
            s = jnp.where(_chunk_of(key) <= _chunk_of(qry & (t - 1)), s, NEG)
        s_sc[slot] = s
        cm_sc[slot] = jnp.max(s, axis=0, keepdims=True)

    def C(i, j, slot):
        vtj = vt_ref[0, :, pl.ds(pl.multiple_of(j * t, t), t)]
        _softmax_update_t(s_sc[slot], cm_sc[slot], vtj, m_sc.at[i], acc_sc.at[i])

    s_full = lambda n, slot: S(fi_ref[n], fj_ref[n], slot, False)
    c_full = lambda n, slot: C(fi_ref[n], fj_ref[n], slot)
    s_diag = lambda d, slot: S(d, d, slot, True)
    c_diag = lambda d, slot: C(d, d, slot)

    if nfull:
        s_full(0, SLOT_A)
    else:
        s_diag(0, SLOT_A)
    slots = (SLOT_A, SLOT_B)
    done = 0
    for unroll in FULL_UNROLLS:
        trips = max(nfull - 1 - done, 0) // unroll

        def body(it, c, unroll=unroll, base=done):
            n = base + unroll * it
            for r in range(unroll):
                s_full(n + r + 1, slots[(r + 1) % 2])
                c_full(n + r, slots[r % 2])
            return c

        lax.fori_loop(0, trips, body, 0)
        done += unroll * trips
    for r, n in enumerate(range(done, nfull)):
        cur, nxt = slots[r % 2], slots[(r + 1) % 2]
        if n + 1 < nfull:
            s_full(n + 1, nxt)
        else:
            s_diag(0, nxt)
        c_full(n, cur)
    first, second = slots[(nfull - done) % 2], slots[(nfull - done + 1) % 2]

    npairs = (nq - 1) // 2

    def pair(d2, c):
        d = 2 * d2
        s_diag(d + 1, second)
        c_diag(d, first)
        s_diag(d + 2, first)
        c_diag(d + 1, second)
        return c

    lax.fori_loop(0, npairs, pair, 0)
    if nq - 2 * npairs == 2:
        s_diag(nq - 1, second)
        c_diag(nq - 2, first)
        c_diag(nq - 1, second)
    else:
        c_diag(nq - 1, first)

    lam = _lambda(lq1_ref[...], lk1_ref[...], lq2_ref[...], lk2_ref[...])
    gain = sub_ref[...] * (1.0 - LAMBDA_INIT)

    def finish(i, c):
        acc = acc_sc[i]
        on = acc[0:V_DIM, :] * (1.0 / acc[V_DIM:V_DIM + 1, :])
        o = on[:, :t] - lam * on[:, t:]
        inv = lax.rsqrt(jnp.mean(o * o, axis=0, keepdims=True) + EPS)
        o_ref[0, :, pl.ds(pl.multiple_of(i * t, t), t)] = (o * inv * gain).astype(BF16)
        return c

    lax.fori_loop(0, nq, finish, 0)


def _attn_prompt(qt, k, vt, subln, lq1, lk1, lq2, lk2, *, t):
    b, s, _ = k.shape
    nq = s // t
    pairs = [(i, j) for i in range(1, nq) for j in range(i)] or [(0, 0)]
    fi = jnp.asarray([p[0] for p in pairs], jnp.int32)
    fj = jnp.asarray([p[1] for p in pairs], jnp.int32)
    small = lambda a: pl.BlockSpec(a.shape, lambda bi, h, *_: (0,) * a.ndim, pipeline_mode=pl.Buffered(1))
    subln = subln.reshape(V_DIM, 1)
    rows = pl.BlockSpec((1, s, V_DIM), lambda bi, h, *_: (bi, 0, h))
    cols = pl.BlockSpec((1, V_DIM, s), lambda bi, h, *_: (bi, h, 0))
    return pl.pallas_call(
        functools.partial(_attn_prompt_kernel, t=t),
        grid_spec=pltpu.PrefetchScalarGridSpec(
            num_scalar_prefetch=2,
            grid=(b, N_HEADS),
            in_specs=[cols, rows, cols, small(subln), small(lq1), small(lk1), small(lq2), small(lk2)],
            out_specs=cols,
            scratch_shapes=[pltpu.VMEM((nq, V_DIM, 2 * t), BF16), pltpu.VMEM((2, t, 2 * t), F32),
                            pltpu.VMEM((2, 1, 2 * t), F32), pltpu.VMEM((nq, 1, 2 * t), F32),
                            pltpu.VMEM((nq, V_DIM + ONES_ROWS, 2 * t), F32)],
        ),
        out_shape=jax.ShapeDtypeStruct((b, D_MODEL, s), BF16),
        compiler_params=pltpu.CompilerParams(dimension_semantics=("arbitrary",) * 2, vmem_limit_bytes=VMEM_LIMIT),
        name="attn_prompt",
    )(fi, fj, qt, k, vt, subln, lq1, lk1, lq2, lk2)


def _attn_sample_kernel(q_ref, kc_ref, vc_ref, kn_ref, vn_ref, sub_ref, lq1_ref, lk1_ref, lq2_ref, lk2_ref,
                        o_ref, m_sc, l_sc, acc_sc, *, q_start):
    t = q_ref.shape[1]
    tkb = kc_ref.shape[1] // N_HEADS
    j = pl.program_id(1)

    @pl.when(j == 0)
    def _():
        _softmax_init(m_sc, l_sc, acc_sc)

    head_cols = lambda h: slice(h * V_DIM, (h + 1) * V_DIM)
    for h in range(N_HEADS):
        qs = _stack_q(q_ref[0, :, head_cols(h)])
        kh = kc_ref[0, pl.ds(h, tkb, stride=N_HEADS), :].astype(BF16)
        vh = vc_ref[0, pl.ds(h, tkb, stride=N_HEADS), :].astype(BF16)
        _softmax_update(_dot_nt(qs, kh), vh, m_sc.at[h], l_sc.at[h], acc_sc.at[h])

    @pl.when(j == pl.num_programs(1) - 1)
    def _():
        lam = _lambda(lq1_ref[...], lk1_ref[...], lq2_ref[...], lk2_ref[...])
        for h in range(N_HEADS):
            qs = _stack_q(q_ref[0, :, head_cols(h)])
            s_n = _dot_nt(qs, kn_ref[0, :, head_cols(h)])
            row = lax.broadcasted_iota(jnp.int32, s_n.shape, 0)
            col = lax.broadcasted_iota(jnp.int32, s_n.shape, 1)
            s_n = jnp.where(_chunk_of(q_start + col) <= _chunk_of(q_start + (row & (t - 1))), s_n, NEG)
            m_prev = m_sc[h]
            m_fin = jnp.maximum(m_prev, jnp.max(s_n, axis=1, keepdims=True))
            alpha = jnp.exp2(m_prev - m_fin)
            p_n = jnp.exp2(s_n - m_fin[:, :t])
            l = jnp.sum(alpha * l_sc[h], axis=1, keepdims=True) + jnp.sum(p_n, axis=1, keepdims=True)
            acc = alpha * acc_sc[h] + _dot(p_n.astype(BF16), vn_ref[0, :, head_cols(h)])
            o_ref[0, :, head_cols(h)] = _finish(acc, l, t, lam, sub_ref[...])


def _attn_sample(q, cache_k, cache_v, kn, vn, subln, lq1, lk1, lq2, lk2, *, tkb):
    b, t, _ = q.shape
    past = cache_k.shape[1]
    kc = cache_k.reshape(b, past * N_HEADS, V_DIM)
    vc = cache_v.reshape(b, past * N_HEADS, V_DIM)
    small = lambda a: _resident(a.shape)
    rows = pl.BlockSpec((1, t, D_MODEL), lambda bi, j: (bi, 0, 0))
    blk = pl.BlockSpec((1, tkb * N_HEADS, V_DIM), lambda bi, j: (bi, j, 0))
    return pl.pallas_call(
        functools.partial(_attn_sample_kernel, q_start=past),
        grid=(b, past // tkb),
        in_specs=[rows, blk, blk, rows, rows, small(subln), small(lq1), small(lk1), small(lq2), small(lk2)],
        out_specs=rows,
        out_shape=jax.ShapeDtypeStruct((b, t, D_MODEL), BF16),
        scratch_shapes=[pltpu.VMEM((N_HEADS, 2 * t, LANES), F32), pltpu.VMEM((N_HEADS, 2 * t, LANES), F32),
                        pltpu.VMEM((N_HEADS, 2 * t, V_DIM), F32)],
        compiler_params=pltpu.CompilerParams(dimension_semantics=("arbitrary",) * 2, vmem_limit_bytes=VMEM_LIMIT),
        name="attn_sample",
    )(q, kc, vc, kn, vn, subln, lq1, lk1, lq2, lk2)


def _trunk(x, conv_state, cache, w, *, tm_ffn, tm_conv, t_attn):
    b, s, _ = x.shape
    n = b * s
    x2 = x.reshape(n, D_MODEL)
    (h,) = _ffn(x2, w["g_ffn1"][0], w["ffn_in"], w["ffn_out"], (0, 0), tm=tm_ffn)
    conv_args = (h.reshape(b, s, D_MODEL), conv_state, w["g_mix"][0], w["conv_in"], w["conv_w"], w["conv_out"])
    if s < tm_conv:
        h, new_conv = _conv_streams(*conv_args)
    else:
        h, new_conv = _conv(*conv_args, tm=tm_conv)
    x1, k_new, v_new, k_bf, v_bf = _ffn(h.reshape(n, D_MODEL), w["g_ffn2"][0], w["ffn_in"],
                                        w["ffn_out"], (0, 1), tm=tm_ffn, proj="kvt" if cache is None else "kv",
                                        proj_gain=w["g_kv"], proj_w=w["w_kv"], seq=s)
    h1, q_bf = _ffn(x1, w["g_ffn1"][1], w["ffn_in"], w["ffn_out"], (1, 0), tm=tm_ffn,
                    proj="qt" if cache is None else "q", proj_gain=w["g_mix"][1], proj_w=w["w_q"], seq=s)
    shp = (b, s, D_MODEL)
    lam = (w["lq1"], w["lk1"], w["lq2"], w["lk2"])
    if cache is None:
        o = _attn_prompt(q_bf, k_bf.reshape(shp), v_bf, w["subln"], *lam, t=t_attn)
    else:
        o = _attn_sample(q_bf.reshape(shp), cache[0], cache[1], k_bf.reshape(shp), v_bf.reshape(shp),
                         w["subln"], *lam, tkb=min(2048, cache[0].shape[1])).reshape(n, D_MODEL)
    (y,) = _ffn(h1, w["g_ffn2"][1], w["ffn_in"], w["ffn_out"], (1, 1), tm=tm_ffn,
                pre=(o, w["w_o"]), final_gain=w["g_final"])
    kv_shape = (b, s, N_HEADS, V_DIM)
    return y.reshape(shp), new_conv, k_new.reshape(kv_shape), v_new.reshape(kv_shape)


def kernel(x_prompt, x_sample, state_conv, cache_k, cache_v, norm_ffn1, norm_mix, norm_ffn2, ffn_w_in, ffn_w_out,
           conv_w_in, conv_w, conv_w_out, norm_kv, w_k, w_v, w_q, lambda_q1, lambda_k1, lambda_q2, lambda_k2,
           subln, w_o, norm_final):
    depth = norm_ffn1.shape[0]
    row = lambda a: a.reshape(1, -1)
    w = {
        "g_ffn1": [row(norm_ffn1[i]) for i in range(depth)],
        "g_mix": [row(norm_mix[i]) for i in range(depth)],
        "g_ffn2": [row(norm_ffn2[i]) for i in range(depth)],
        "ffn_in": ffn_w_in.astype(BF16),
        "ffn_out": ffn_w_out.astype(BF16),
        "conv_in": conv_w_in[0].astype(BF16),
        "conv_w": conv_w[0],
        "conv_out": conv_w_out[0].astype(BF16),
        "g_kv": row(norm_kv),
        "w_kv": jnp.concatenate([w_k, w_v], axis=1).astype(BF16),
        "w_q": w_q[0].astype(BF16),
        "lq1": row(lambda_q1[0]), "lk1": row(lambda_k1[0]), "lq2": row(lambda_q2[0]), "lk2": row(lambda_k2[0]),
        "subln": row(subln[0]),
        "w_o": w_o[0].astype(BF16),
        "g_final": row(norm_final),
    }
    b = x_prompt.shape[0]
    conv0 = jnp.zeros((b, 1, CONV_W - 1, D_MODEL), x_prompt.dtype)
    y_p, conv_p, k_p, v_p = _trunk(x_prompt, conv0, None, w, tm_ffn=512, tm_conv=512, t_attn=512)
    y_s, conv_s, k_s, v_s = _trunk(x_sample, state_conv, (cache_k, cache_v), w, tm_ffn=512, tm_conv=512,
                                   t_attn=None)
    return (y_p, y_s, conv_p, k_p, v_p, conv_s, k_s, v_s)
```

```python
import functools
import math

import jax
import jax.numpy as jnp
from jax import lax
from jax.experimental import pallas as pl
from jax.experimental.pallas import tpu as pltpu

D_MODEL = 1024
CHUNK = 64
CONV_W = 3
D_FF = 2816
N_HEADS = 8
HEAD_DIM = 64
V_DIM = 2 * HEAD_DIM
EPS = 1e-6
NEG = -1e30
LAMBDA_INIT = 0.8 - 0.6 * math.exp(-0.3 * 1)
LOG2E = math.log2(math.e)
Q_SCALE = HEAD_DIM ** -0.5 * LOG2E

V7X_VMEM_BYTES = 64 * 1024 * 1024
VMEM_LIMIT = V7X_VMEM_BYTES - 8 * 1024 * 1024
LANES = 128

BF16 = jnp.bfloat16
F32 = jnp.float32


def _resident(shape):
    nd = len(shape)
    return pl.BlockSpec(shape, lambda *_: (0,) * nd, pipeline_mode=pl.Buffered(1))


def _resident_slice(shape, lead):
    tail = tuple(shape[len(lead):])
    index = tuple(lead) + (0,) * len(tail)
    return pl.BlockSpec((None,) * len(lead) + tail, lambda *_: index, pipeline_mode=pl.Buffered(1))


def _rms(x, g):
    return x * lax.rsqrt(jnp.mean(x * x, axis=-1, keepdims=True) + EPS) * g


def _dot(a, b):
    return jnp.dot(a, b, preferred_element_type=F32)


def _dot_nt(a, b):
    return lax.dot_general(a, b, (((1,), (1,)), ((), ())), preferred_element_type=F32)


def _ffn_kernel(*refs, pre_proj, final_norm, proj):
    refs = list(refs)
    x_ref = refs.pop(0)
    if pre_proj:
        a_ref, wpre_ref = refs.pop(0), refs.pop(0)
    g_ref, win_ref, wout_ref = refs.pop(0), refs.pop(0), refs.pop(0)
    if final_norm:
        gf_ref = refs.pop(0)
    if proj is not None:
        gp_ref, wp_ref = refs.pop(0), refs.pop(0)
    y_ref = refs.pop(0)

    x = x_ref[...]
    if pre_proj == "rows":
        x = x + _dot(a_ref[...], wpre_ref[...])
    elif pre_proj == "cols":
        x = x + lax.dot_general(a_ref[0], wpre_ref[...], (((0,), (0,)), ((), ())), preferred_element_type=F32)
    xn = _rms(x, g_ref[...]).astype(BF16)
    gu = _dot(xn, win_ref[...])
    g, u = gu[:, :D_FF], gu[:, D_FF:]
    act = (g * (1.0 / (1.0 + jnp.exp(-g))) * u).astype(BF16)
    y = x + 0.5 * _dot(act, wout_ref[...])
    y_ref[...] = _rms(y, gf_ref[...]) if final_norm else y

    if proj in ("kv", "kvt"):
        kvn = _rms(y, gp_ref[...]).astype(BF16)
        r = _dot(kvn, wp_ref[...])
        k, v = r[:, :D_MODEL], r[:, D_MODEL:]
        if proj == "kv":
            kf_ref, vf_ref, kb_ref, vb_ref = refs
            vb_ref[...] = v.astype(BF16)
        else:
            kf_ref, vf_ref, kb_ref, vtb_ref = refs
            vtb_ref[0] = v.T.astype(BF16)
        kf_ref[...] = k
        vf_ref[...] = v
        kb_ref[...] = k.astype(BF16)
    elif proj in ("q", "qt"):
        (qb_ref,) = refs
        q = _dot(_rms(y, gp_ref[...]).astype(BF16), wp_ref[...]) * Q_SCALE
        if proj == "q":
            qb_ref[...] = q.astype(BF16)
        else:
            qb_ref[0] = q.T.astype(BF16)


def _ffn(x, gain, w_in, w_out, layer, *, tm, pre=None, final_gain=None, proj=None, proj_gain=None, proj_w=None,
         seq=None):
    n = x.shape[0]
    tm = min(tm, n)
    row = lambda cols: pl.BlockSpec((tm, cols), lambda i: (i, 0))
    args, specs = [x], [row(D_MODEL)]
    pre_kind = None
    if pre is not None:
        a, w_pre = pre
        args += [a, w_pre]
        if a.ndim == 2:
            pre_kind = "rows"
            specs += [row(D_MODEL), _resident(w_pre.shape)]
        else:
            pre_kind = "cols"
            nsb_a = a.shape[2] // tm
            specs += [pl.BlockSpec((1, D_MODEL, tm), lambda i: (i // nsb_a, 0, i % nsb_a)), _resident(w_pre.shape)]
    args += [gain, w_in, w_out]
    specs += [_resident(gain.shape), _resident_slice(w_in.shape, layer), _resident_slice(w_out.shape, layer)]
    if final_gain is not None:
        args.append(final_gain)
        specs.append(_resident(final_gain.shape))
    out_shape = [jax.ShapeDtypeStruct((n, D_MODEL), F32)]
    out_specs = [row(D_MODEL)]
    if proj is not None:
        args += [proj_gain, proj_w]
        specs += [_resident(proj_gain.shape), _resident(proj_w.shape)]
        dts = {"kv": (F32, F32, BF16, BF16), "kvt": (F32, F32, BF16), "q": (BF16,), "qt": ()}[proj]
        out_shape += [jax.ShapeDtypeStruct((n, D_MODEL), dt) for dt in dts]
        out_specs += [row(D_MODEL) for _ in dts]
        if proj in ("kvt", "qt"):
            nsb = seq // tm
            out_shape.append(jax.ShapeDtypeStruct((n // seq, D_MODEL, seq), BF16))
            out_specs.append(pl.BlockSpec((1, D_MODEL, tm), lambda i: (i // nsb, 0, i % nsb)))
    return pl.pallas_call(
        functools.partial(_ffn_kernel, pre_proj=pre_kind, final_norm=final_gain is not None, proj=proj),
        grid=(n // tm,),
        in_specs=specs,
        out_specs=out_specs,
        out_shape=out_shape,
        compiler_params=pltpu.CompilerParams(dimension_semantics=("arbitrary",), vmem_limit_bytes=VMEM_LIMIT),
        name="ffn" + ("_pre" if pre is not None else "") + ("_" + proj if proj else "") + ("_fin" if final_gain is not None else ""),
    )(*args)


def _conv_kernel(x_ref, st_ref, g_ref, win_ref, cw_ref, wout_ref, o_ref, ns_ref, carry):
    tm = x_ref.shape[1]

    @pl.when(pl.program_id(1) == 0)
    def _():
        carry[0:CONV_W - 1, :] = st_ref[0, 0]

    h = x_ref[0]
    hn = _rms(h, g_ref[...]).astype(BF16)
    proj = _dot(hn, win_ref[...])
    b_g, c_g, hh = proj[:, :D_MODEL], proj[:, D_MODEL:2 * D_MODEL], proj[:, 2 * D_MODEL:]
    u = c_g * hh
    prev0, prev1 = carry[0:1, :], carry[1:2, :]
    row = lax.broadcasted_iota(jnp.int32, (tm, D_MODEL), 0)
    r1 = pltpu.roll(u, 1, 0)
    r2 = pltpu.roll(u, 2, 0)
    u1 = jnp.where(row == 0, prev1, r1)
    u2 = jnp.where(row == 0, prev0, jnp.where(row == 1, prev1, r2))
    cw = cw_ref[...]
    conv = cw[0:1, :] * u2 + cw[1:2, :] * u1 + cw[2:3, :] * u
    o_ref[0] = h + _dot((b_g * conv).astype(BF16), wout_ref[...])
    tail = r2[0:CONV_W - 1, :]
    carry[0:CONV_W - 1, :] = tail
    ns_ref[0, 0] = tail


def _conv_streams_kernel(x_ref, st_ref, g_ref, win_ref, cw_ref, wout_ref, o_ref, ns_ref, *, s):
    n = x_ref.shape[0]
    b = n // s
    h = x_ref[...]
    hn = _rms(h, g_ref[...]).astype(BF16)
    proj = _dot(hn, win_ref[...])
    b_g, c_g, hh = proj[:, :D_MODEL], proj[:, D_MODEL:2 * D_MODEL], proj[:, 2 * D_MODEL:]
    u = c_g * hh
    st = st_ref[...]
    per_row = lambda rows: jnp.broadcast_to(rows, (b, s, D_MODEL)).reshape(n, D_MODEL)
    prev0, prev1 = per_row(st[:, 0:1, :]), per_row(st[:, 1:2, :])
    pos = lax.broadcasted_iota(jnp.int32, (n, D_MODEL), 0) & (s - 1)
    u1 = jnp.where(pos == 0, prev1, pltpu.roll(u, 1, 0))
    u2 = jnp.where(pos == 0, prev0, jnp.where(pos == 1, prev1, pltpu.roll(u, 2, 0)))
    cw = cw_ref[...]
    conv = cw[0:1, :] * u2 + cw[1:2, :] * u1 + cw[2:3, :] * u
    o_ref[...] = h + _dot((b_g * conv).astype(BF16), wout_ref[...])
    ns_ref[...] = pltpu.roll(u, n - (s - (CONV_W - 1)), 0).reshape(b, s, D_MODEL)[:, 0:CONV_W - 1, :]


def _conv_streams(x, state, gain, w_in, conv_w, w_out):
    b, s, _ = x.shape
    assert s & (s - 1) == 0 and s % 8 == 0, s
    n = b * s
    full = lambda shape: pl.BlockSpec(shape, lambda i: (0,) * len(shape))
    y, ns = pl.pallas_call(
        functools.partial(_conv_streams_kernel, s=s),
        grid=(1,),
        in_specs=[full((n, D_MODEL)), full((b, CONV_W - 1, D_MODEL)),
                  full(gain.shape), full(w_in.shape), full(conv_w.shape), full(w_out.shape)],
        out_specs=[full((n, D_MODEL)), full((b, CONV_W - 1, D_MODEL))],
        out_shape=[jax.ShapeDtypeStruct((n, D_MODEL), F32), jax.ShapeDtypeStruct((b, CONV_W - 1, D_MODEL), F32)],
        compiler_params=pltpu.CompilerParams(dimension_semantics=("arbitrary",), vmem_limit_bytes=VMEM_LIMIT),
        name="conv_streams",
    )(x.reshape(n, D_MODEL), state.reshape(b, CONV_W - 1, D_MODEL), gain, w_in, conv_w, w_out)
    return y.reshape(b, s, D_MODEL), ns.reshape(b, 1, CONV_W - 1, D_MODEL)


def _conv(x, state, gain, w_in, conv_w, w_out, *, tm):
    b, s, _ = x.shape
    tm = min(tm, s)
    return pl.pallas_call(
        _conv_kernel,
        grid=(b, s // tm),
        in_specs=[
            pl.BlockSpec((1, tm, D_MODEL), lambda bi, i: (bi, i, 0)),
            pl.BlockSpec((1, 1, CONV_W - 1, D_MODEL), lambda bi, i: (bi, 0, 0, 0)),
            _resident(gain.shape), _resident(w_in.shape), _resident(conv_w.shape), _resident(w_out.shape),
        ],
        out_specs=[
            pl.BlockSpec((1, tm, D_MODEL), lambda bi, i: (bi, i, 0)),
            pl.BlockSpec((1, 1, CONV_W - 1, D_MODEL), lambda bi, i: (bi, 0, 0, 0)),
        ],
        out_shape=[jax.ShapeDtypeStruct((b, s, D_MODEL), F32),
                   jax.ShapeDtypeStruct((b, 1, CONV_W - 1, D_MODEL), F32)],
        scratch_shapes=[pltpu.VMEM((8, D_MODEL), F32)],
        compiler_params=pltpu.CompilerParams(dimension_semantics=("arbitrary", "arbitrary"),
                                             vmem_limit_bytes=VMEM_LIMIT),
        name="conv",
    )(x, state, gain, w_in, conv_w, w_out)


def _stack_q(q):
    lane = lax.broadcasted_iota(jnp.int32, q.shape, 1)
    zero = jnp.zeros_like(q)
    return jnp.concatenate([jnp.where(lane < HEAD_DIM, q, zero), jnp.where(lane >= HEAD_DIM, q, zero)], axis=0)


def _chunk_of(pos):
    return lax.shift_right_logical(pos, CHUNK.bit_length() - 1)


def _lambda(lq1, lk1, lq2, lk2):
    return (jnp.exp(jnp.sum(lq1 * lk1, axis=-1, keepdims=True))
            - jnp.exp(jnp.sum(lq2 * lk2, axis=-1, keepdims=True)) + LAMBDA_INIT)


def _finish(acc, l, t, lam, subln):
    o = acc[:t] / l[:t] - lam * (acc[t:] / l[t:])
    return (_rms(o, subln) * (1.0 - LAMBDA_INIT)).astype(BF16)


def _softmax_init(m_ref, l_ref, acc_ref):
    m_ref[...] = jnp.full(m_ref.shape, NEG, F32)
    l_ref[...] = jnp.zeros(l_ref.shape, F32)
    acc_ref[...] = jnp.zeros(acc_ref.shape, F32)


def _softmax_update(s, v, m_ref, l_ref, acc_ref):
    cols = [s[:, c * LANES:(c + 1) * LANES] for c in range(s.shape[1] // LANES)]
    m_prev = m_ref[...]
    m_next = jnp.maximum(m_prev, jnp.max(functools.reduce(jnp.maximum, cols), axis=1, keepdims=True))
    alpha = jnp.exp2(m_prev - m_next)
    ps = [jnp.exp2(c - m_next) for c in cols]
    l_ref[...] = alpha * l_ref[...] + functools.reduce(jnp.add, ps)
    p = jnp.concatenate([c.astype(BF16) for c in ps], axis=1)
    acc_ref[...] = alpha * acc_ref[...] + _dot(p, v)
    m_ref[...] = m_next


ONES_ROWS = 16


def _softmax_update_t(s, cmax, vt, m_ref, acc_ref):
    m_prev = m_ref[...]
    m_next = jnp.maximum(m_prev, cmax)
    alpha = jnp.exp2(m_prev - m_next)
    p = jnp.exp2(s - m_next).astype(BF16)
    vt1 = jnp.concatenate([vt, jnp.ones((ONES_ROWS, vt.shape[1]), BF16)], axis=0)
    acc_ref[...] = alpha * acc_ref[...] + _dot(vt1, p)
    m_ref[...] = m_next


SLOT_A, SLOT_B = 0, 1
FULL_UNROLLS = (8, 4)
DIAG_UNROLLS = (4, 2)


def _attn_prompt_kernel(fi_ref, fj_ref, qt_ref, k_ref, vt_ref, sub_ref, lq1_ref, lk1_ref, lq2_ref, lk2_ref, o_ref,
                        qs_sc, s_sc, cm_sc, m_sc, acc_sc, *, t):
    nq = qt_ref.shape[2] // t
    nfull = nq * (nq - 1) // 2
    feat = lax.broadcasted_iota(jnp.int32, (V_DIM, t), 0)
    for i in range(nq):
        qt = qt_ref[0, :, i * t:(i + 1) * t]
        zero = jnp.zeros_like(qt)
        qs_sc[i] = jnp.concatenate([jnp.where(feat < HEAD_DIM, qt, zero), jnp.where(feat >= HEAD_DIM, qt, zero)],
                                   axis=1)
    m_sc[...] = jnp.full(m_sc.shape, NEG, F32)
    acc_sc[...] = jnp.zeros(acc_sc.shape, F32)

    def S(i, j, slot, masked):
        kj = k_ref[0, pl.ds(pl.multiple_of(j * t, t), t), :]
        s = _dot(kj, qs_sc[i])
        if masked:
            key = lax.broadcasted_iota(jnp.int32, s.shape, 0)
            qry = lax.broadcasted_iota(jnp.int32, s.shape, 1)
            s = jnp.where(_chunk_of(key) <= _chunk_of(qry & (t - 1)), s, NEG)
        s_sc[slot] = s
        cm_sc[slot] = jnp.max(s, axis=0, keepdims=True)

    def C(i, j, slot):
        vtj = vt_ref[0, :, pl.ds(pl.multiple_of(j * t, t), t)]
        _softmax_update_t(s_sc[slot], cm_sc[slot], vtj, m_sc.at[i], acc_sc.at[i])

    s_full = lambda n, slot: S(fi_ref[n], fj_ref[n], slot, False)
    c_full = lambda n, slot: C(fi_ref[n], fj_ref[n], slot)
    s_diag = lambda d, slot: S(d, d, slot, True)
    c_diag = lambda d, slot: C(d, d, slot)

    if nfull:
        s_full(0, SLOT_A)
    else:
        s_diag(0, SLOT_A)
    slots = (SLOT_A, SLOT_B)
    done = 0
    for unroll in FULL_UNROLLS:
        trips = max(nfull - 1 - done, 0) // unroll

        def body(it, c, unroll=unroll, base=done):
            n = base + unroll * it
            for r in range(unroll):
                s_full(n + r + 1, slots[(r + 1) % 2])
                c_full(n + r, slots[r % 2])
            return c

        lax.fori_loop(0, trips, body, 0)
        done += unroll * trips
    for r, n in enumerate(range(done, nfull)):
        cur, nxt = slots[r % 2], slots[(r + 1) % 2]
        if n + 1 < nfull:
            s_full(n + 1, nxt)
        else:
            s_diag(0, nxt)
        c_full(n, cur)
    first, second = slots[(nfull - done) % 2], slots[(nfull - done + 1) % 2]

    slots = (first, second)
    done = 0
    for unroll in DIAG_UNROLLS:
        trips = max(nq - 1 - done, 0) // unroll

        def body(it, c, unroll=unroll, base=done):
            d = base + unroll * it
            for r in range(unroll):
                s_diag(d + r + 1, slots[(r + 1) % 2])
                c_diag(d + r, slots[r % 2])
            return c

        lax.fori_loop(0, trips, body, 0)
        done += unroll * trips
    for r, d in enumerate(range(done, nq)):
        if d + 1 < nq:
            s_diag(d + 1, slots[(r + 1) % 2])
        c_diag(d, slots[r % 2])

    lam = _lambda(lq1_ref[...], lk1_ref[...], lq2_ref[...], lk2_ref[...])
    gain = sub_ref[...] * (1.0 - LAMBDA_INIT)

    def finish(i, c):
        acc = acc_sc[i]
        on = acc[0:V_DIM, :] * (1.0 / acc[V_DIM:V_DIM + 1, :])
        o = on[:, :t] - lam * on[:, t:]
        inv = lax.rsqrt(jnp.mean(o * o, axis=0, keepdims=True) + EPS)
        o_ref[0, :, pl.ds(pl.multiple_of(i * t, t), t)] = (o * inv * gain).astype(BF16)
        return c

    lax.fori_loop(0, nq, finish, 0)


def _attn_prompt(qt, k, vt, subln, lq1, lk1, lq2, lk2, *, t):
    b, s, _ = k.shape
    nq = s // t
    pairs = [(i, j) for i in range(1, nq) for j in range(i)] or [(0, 0)]
    fi = jnp.asarray([p[0] for p in pairs], jnp.int32)
    fj = jnp.asarray([p[1] for p in pairs], jnp.int32)
    small = lambda a: pl.BlockSpec(a.shape, lambda bi, h, *_: (0,) * a.ndim, pipeline_mode=pl.Buffered(1))
    subln = subln.reshape(V_DIM, 1)
    rows = pl.BlockSpec((1, s, V_DIM), lambda bi, h, *_: (bi, 0, h))
    cols = pl.BlockSpec((1, V_DIM, s), lambda bi, h, *_: (bi, h, 0))
    return pl.pallas_call(
        functools.partial(_attn_prompt_kernel, t=t),
        grid_spec=pltpu.PrefetchScalarGridSpec(
            num_scalar_prefetch=2,
            grid=(b, N_HEADS),
            in_specs=[cols, rows, cols, small(subln), small(lq1), small(lk1), small(lq2), small(lk2)],
            out_specs=cols,
            scratch_shapes=[pltpu.VMEM((nq, V_DIM, 2 * t), BF16), pltpu.VMEM((2, t, 2 * t), F32),
                            pltpu.VMEM((2, 1, 2 * t), F32), pltpu.VMEM((nq, 1, 2 * t), F32),
                            pltpu.VMEM((nq, V_DIM + ONES_ROWS, 2 * t), F32)],
        ),
        out_shape=jax.ShapeDtypeStruct((b, D_MODEL, s), BF16),
        compiler_params=pltpu.CompilerParams(dimension_semantics=("arbitrary",) * 2, vmem_limit_bytes=VMEM_LIMIT),
        name="attn_prompt",
    )(fi, fj, qt, k, vt, subln, lq1, lk1, lq2, lk2)


def _attn_sample_kernel(q_ref, kc_ref, vc_ref, kn_ref, vn_ref, sub_ref, lq1_ref, lk1_ref, lq2_ref, lk2_ref,
                        o_ref, m_sc, l_sc, acc_sc, *, q_start):
    t = q_ref.shape[1]
    tkb = kc_ref.shape[1] // N_HEADS
    j = pl.program_id(1)

    @pl.when(j == 0)
    def _():
        _softmax_init(m_sc, l_sc, acc_sc)

    head_cols = lambda h: slice(h * V_DIM, (h + 1) * V_DIM)
    scores = []
    for h in range(N_HEADS):
        qs = _stack_q(q_ref[0, :, head_cols(h)])
        kh = kc_ref[0, pl.ds(h, tkb, stride=N_HEADS), :].astype(BF16)
        scores.append(_dot_nt(qs, kh))
    for h in range(N_HEADS):
        vh = vc_ref[0, pl.ds(h, tkb, stride=N_HEADS), :].astype(BF16)
        _softmax_update(scores[h], vh, m_sc.at[h], l_sc.at[h], acc_sc.at[h])

    @pl.when(j == pl.num_programs(1) - 1)
    def _():
        lam = _lambda(lq1_ref[...], lk1_ref[...], lq2_ref[...], lk2_ref[...])
        for h in range(N_HEADS):
            qs = _stack_q(q_ref[0, :, head_cols(h)])
            s_n = _dot_nt(qs, kn_ref[0, :, head_cols(h)])
            row = lax.broadcasted_iota(jnp.int32, s_n.shape, 0)
            col = lax.broadcasted_iota(jnp.int32, s_n.shape, 1)
            s_n = jnp.where(_chunk_of(q_start + col) <= _chunk_of(q_start + (row & (t - 1))), s_n, NEG)
            m_prev = m_sc[h]
            m_fin = jnp.maximum(m_prev, jnp.max(s_n, axis=1, keepdims=True))
            alpha = jnp.exp2(m_prev - m_fin)
            p_n = jnp.exp2(s_n - m_fin[:, :t])
            l = jnp.sum(alpha * l_sc[h], axis=1, keepdims=True) + jnp.sum(p_n, axis=1, keepdims=True)
            acc = alpha * acc_sc[h] + _dot(p_n.astype(BF16), vn_ref[0, :, head_cols(h)])
            o_ref[0, :, head_cols(h)] = _finish(acc, l, t, lam, sub_ref[...])


def _attn_sample(q, cache_k, cache_v, kn, vn, subln, lq1, lk1, lq2, lk2, *, tkb):
    b, t, _ = q.shape
    past = cache_k.shape[1]
    kc = cache_k.reshape(b, past * N_HEADS, V_DIM)
    vc = cache_v.reshape(b, past * N_HEADS, V_DIM)
    small = lambda a: _resident(a.shape)
    rows = pl.BlockSpec((1, t, D_MODEL), lambda bi, j: (bi, 0, 0))
    blk = pl.BlockSpec((1, tkb * N_HEADS, V_DIM), lambda bi, j: (bi, j, 0))
    return pl.pallas_call(
        functools.partial(_attn_sample_kernel, q_start=past),
        grid=(b, past // tkb),
        in_specs=[rows, blk, blk, rows, rows, small(subln), small(lq1), small(lk1), small(lq2), small(lk2)],
        out_specs=rows,
        out_shape=jax.ShapeDtypeStruct((b, t, D_MODEL), BF16),
        scratch_shapes=[pltpu.VMEM((N_HEADS, 2 * t, LANES), F32), pltpu.VMEM((N_HEADS, 2 * t, LANES), F32),
                        pltpu.VMEM((N_HEADS, 2 * t, V_DIM), F32)],
        compiler_params=pltpu.CompilerParams(dimension_semantics=("arbitrary",) * 2, vmem_limit_bytes=VMEM_LIMIT),
        name="attn_sample",
    )(q, kc, vc, kn, vn, subln, lq1, lk1, lq2, lk2)


def _trunk(x, conv_state, cache, w, *, tm_ffn, tm_conv, t_attn):
    b, s, _ = x.shape
    n = b * s
    x2 = x.reshape(n, D_MODEL)
    (h,) = _ffn(x2, w["g_ffn1"][0], w["ffn_in"], w["ffn_out"], (0, 0), tm=tm_ffn)
    conv_args = (h.reshape(b, s, D_MODEL), conv_state, w["g_mix"][0], w["conv_in"], w["conv_w"], w["conv_out"])
    if s < tm_conv:
        h, new_conv = _conv_streams(*conv_args)
    else:
        h, new_conv = _conv(*conv_args, tm=tm_conv)
    x1, k_new, v_new, k_bf, v_bf = _ffn(h.reshape(n, D_MODEL), w["g_ffn2"][0], w["ffn_in"],
                                        w["ffn_out"], (0, 1), tm=tm_ffn, proj="kvt" if cache is None else "kv",
                                        proj_gain=w["g_kv"], proj_w=w["w_kv"], seq=s)
    h1, q_bf = _ffn(x1, w["g_ffn1"][1], w["ffn_in"], w["ffn_out"], (1, 0), tm=tm_ffn,
                    proj="qt" if cache is None else "q", proj_gain=w["g_mix"][1], proj_w=w["w_q"], seq=s)
    shp = (b, s, D_MODEL)
    lam = (w["lq1"], w["lk1"], w["lq2"], w["lk2"])
    if cache is None:
        o = _attn_prompt(q_bf, k_bf.reshape(shp), v_bf, w["subln"], *lam, t=t_attn)
    else:
        o = _attn_sample(q_bf.reshape(shp), cache[0], cache[1], k_bf.reshape(shp), v_bf.reshape(shp),
                         w["subln"], *lam, tkb=min(2048, cache[0].shape[1])).reshape(n, D_MODEL)
    (y,) = _ffn(h1, w["g_ffn2"][1], w["ffn_in"], w["ffn_out"], (1, 1), tm=tm_ffn,
                pre=(o, w["w_o"]), final_gain=w["g_final"])
    kv_shape = (b, s, N_HEADS, V_DIM)
    return y.reshape(shp), new_conv, k_new.reshape(kv_shape), v_new.reshape(kv_shape)


def kernel(x_prompt, x_sample, state_conv, cache_k, cache_v, norm_ffn1, norm_mix, norm_ffn2, ffn_w_in, ffn_w_out,
           conv_w_in, conv_w, conv_w_out, norm_kv, w_k, w_v, w_q, lambda_q1, lambda_k1, lambda_q2, lambda_k2,
           subln, w_o, norm_final):
    depth = norm_ffn1.shape[0]
    row = lambda a: a.reshape(1, -1)
    w = {
        "g_ffn1": [row(norm_ffn1[i]) for i in range(depth)],
        "g_mix": [row(norm_mix[i]) for i in range(depth)],
        "g_ffn2": [row(norm_ffn2[i]) for i in range(depth)],
        "ffn_in": ffn_w_in.astype(BF16),
        "ffn_out": ffn_w_out.astype(BF16),
        "conv_in": conv_w_in[0].astype(BF16),
        "conv_w": conv_w[0],
        "conv_out": conv_w_out[0].astype(BF16),
        "g_kv": row(norm_kv),
        "w_kv": jnp.concatenate([w_k, w_v], axis=1).astype(BF16),
        "w_q": w_q[0].astype(BF16),
        "lq1": row(lambda_q1[0]), "lk1": row(lambda_k1[0]), "lq2": row(lambda_q2[0]), "lk2": row(lambda_k2[0]),
        "subln": row(subln[0]),
        "w_o": w_o[0].astype(BF16),
        "g_final": row(norm_final),
    }
    b = x_prompt.shape[0]
    conv0 = jnp.zeros((b, 1, CONV_W - 1, D_MODEL), x_prompt.dtype)
    y_p, conv_p, k_p, v_p = _trunk(x_prompt, conv0, None, w, tm_ffn=512, tm_conv=512, t_attn=512)
    y_s, conv_s, k_s, v_s = _trunk(x_sample, state_conv, (cache_k, cache_v), w, tm_ffn=512, tm_conv=512,
                                   t_attn=None)
    return (y_p, y_s, conv_p, k_p, v_p, conv_s, k_s, v_s)
```

```python
import functools
import math

import jax
import jax.numpy as jnp
from jax import lax
from jax.experimental import pallas as pl
from jax.experimental.pallas import tpu as pltpu

D_MODEL = 1024
CHUNK = 64
CONV_W = 3
D_FF = 2816
N_HEADS = 8
HEAD_DIM = 64
V_DIM = 2 * HEAD_DIM
EPS = 1e-6
NEG = -1e30
LAMBDA_INIT = 0.8 - 0.6 * math.exp(-0.3 * 1)
LOG2E = math.log2(math.e)
Q_SCALE = HEAD_DIM ** -0.5 * LOG2E

V7X_VMEM_BYTES = 64 * 1024 * 1024
VMEM_LIMIT = V7X_VMEM_BYTES - 8 * 1024 * 1024
LANES = 128

BF16 = jnp.bfloat16
F32 = jnp.float32


def _resident(shape):
    nd = len(shape)
    return pl.BlockSpec(shape, lambda *_: (0,) * nd, pipeline_mode=pl.Buffered(1))


BF16_SUBLANES = 16


def _cast_jobs(jobs, nsteps, step_of):
    args, in_specs, out_shapes, out_specs = [], [], [], []
    for array, lead in jobs:
        rows, cols = array.shape[len(lead):]
        nblk = nsteps
        while rows % nblk or (rows // nblk) % BF16_SUBLANES:
            assert nblk % 2 == 0, (array.shape, nsteps)
            nblk //= 2
        rep, br = nsteps // nblk, rows // nblk
        args.append(array)
        in_specs.append(pl.BlockSpec((None,) * len(lead) + (br, cols),
                                     lambda *ids, lead=tuple(lead), rep=rep: lead + (step_of(*ids) // rep, 0)))
        out_shapes.append(jax.ShapeDtypeStruct((rows, cols), BF16))
        out_specs.append(pl.BlockSpec((br, cols), lambda *ids, rep=rep: (step_of(*ids) // rep, 0)))
    return args, in_specs, out_shapes, out_specs


def _run_casts(src_refs, dst_refs):
    for src, dst in zip(src_refs, dst_refs):
        dst[...] = src[...].astype(BF16)


def _rms(x, g):
    return x * lax.rsqrt(jnp.mean(x * x, axis=-1, keepdims=True) + EPS) * g


def _dot(a, b):
    return jnp.dot(a, b, preferred_element_type=F32)


def _dot_nt(a, b):
    return lax.dot_general(a, b, (((1,), (1,)), ((), ())), preferred_element_type=F32)


def _ffn_kernel(*refs, pre_proj, final_norm, proj, n_cast):
    refs = list(refs)
    x_ref = refs.pop(0)
    if pre_proj:
        a_ref, wpre_ref = refs.pop(0), refs.pop(0)
    g_ref, win_ref, wout_ref = refs.pop(0), refs.pop(0), refs.pop(0)
    if final_norm:
        gf_ref = refs.pop(0)
    if proj is not None:
        gp_ref, wp_ref = refs.pop(0), refs.pop(0)
    cast_src = [refs.pop(0) for _ in range(n_cast)]
    y_ref = refs.pop(0)
    cast_dst = [refs.pop() for _ in range(n_cast)][::-1]
    _run_casts(cast_src, cast_dst)

    x = x_ref[...]
    if pre_proj == "rows":
        x = x + _dot(a_ref[...], wpre_ref[...])
    elif pre_proj == "cols":
        x = x + lax.dot_general(a_ref[0], wpre_ref[...], (((0,), (0,)), ((), ())), preferred_element_type=F32)
    xn = _rms(x, g_ref[...]).astype(BF16)
    gu = _dot(xn, win_ref[...])
    g, u = gu[:, :D_FF], gu[:, D_FF:]
    act = (g * (1.0 / (1.0 + jnp.exp(-g))) * u).astype(BF16)
    y = x + 0.5 * _dot(act, wout_ref[...])
    y_ref[...] = _rms(y, gf_ref[...]) if final_norm else y

    if proj in ("kv", "kvt"):
        kvn = _rms(y, gp_ref[...]).astype(BF16)
        r = _dot(kvn, wp_ref[...])
        k, v = r[:, :D_MODEL], r[:, D_MODEL:]
        if proj == "kv":
            kf_ref, vf_ref, kb_ref, vb_ref = refs
            vb_ref[...] = v.astype(BF16)
        else:
            kf_ref, vf_ref, kb_ref, vtb_ref = refs
            vtb_ref[0] = v.T.astype(BF16)
        kf_ref[...] = k
        vf_ref[...] = v
        kb_ref[...] = k.astype(BF16)
    elif proj in ("q", "qt"):
        (qb_ref,) = refs
        q = _dot(_rms(y, gp_ref[...]).astype(BF16), wp_ref[...]) * Q_SCALE
        if proj == "q":
            qb_ref[...] = q.astype(BF16)
        else:
            qb_ref[0] = q.T.astype(BF16)


def _ffn(x, gain, w_in, w_out, *, tm, pre=None, final_gain=None, proj=None, proj_gain=None, proj_w=None,
         seq=None, casts=()):
    n = x.shape[0]
    tm = min(tm, n)
    row = lambda cols: pl.BlockSpec((tm, cols), lambda i: (i, 0))
    args, specs = [x], [row(D_MODEL)]
    pre_kind = None
    if pre is not None:
        a, w_pre = pre
        args += [a, w_pre]
        if a.ndim == 2:
            pre_kind = "rows"
            specs += [row(D_MODEL), _resident(w_pre.shape)]
        else:
            pre_kind = "cols"
            nsb_a = a.shape[2] // tm
            specs += [pl.BlockSpec((1, D_MODEL, tm), lambda i: (i // nsb_a, 0, i % nsb_a)), _resident(w_pre.shape)]
    args += [gain, w_in, w_out]
    specs += [_resident(gain.shape), _resident(w_in.shape), _resident(w_out.shape)]
    if final_gain is not None:
        args.append(final_gain)
        specs.append(_resident(final_gain.shape))
    out_shape = [jax.ShapeDtypeStruct((n, D_MODEL), F32)]
    out_specs = [row(D_MODEL)]
    if proj is not None:
        args += [proj_gain, proj_w]
        specs += [_resident(proj_gain.shape), _resident(proj_w.shape)]
        dts = {"kv": (F32, F32, BF16, BF16), "kvt": (F32, F32, BF16), "q": (BF16,), "qt": ()}[proj]
        out_shape += [jax.ShapeDtypeStruct((n, D_MODEL), dt) for dt in dts]
        out_specs += [row(D_MODEL) for _ in dts]
        if proj in ("kvt", "qt"):
            nsb = seq // tm
            out_shape.append(jax.ShapeDtypeStruct((n // seq, D_MODEL, seq), BF16))
            out_specs.append(pl.BlockSpec((1, D_MODEL, tm), lambda i: (i // nsb, 0, i % nsb)))
    c_args, c_in, c_shapes, c_out = _cast_jobs(casts, n // tm, lambda i: i)
    args += c_args
    specs += c_in
    out_shape += c_shapes
    out_specs += c_out
    return pl.pallas_call(
        functools.partial(_ffn_kernel, pre_proj=pre_kind, final_norm=final_gain is not None, proj=proj,
                          n_cast=len(casts)),
        grid=(n // tm,),
        in_specs=specs,
        out_specs=out_specs,
        out_shape=out_shape,
        compiler_params=pltpu.CompilerParams(dimension_semantics=("arbitrary",), vmem_limit_bytes=VMEM_LIMIT),
        name="ffn" + ("_pre" if pre is not None else "") + ("_" + proj if proj else "") + ("_fin" if final_gain is not None else ""),
    )(*args)


def _conv_kernel(x_ref, st_ref, g_ref, win_ref, cw_ref, wout_ref, *refs, n_cast):
    cast_src, (o_ref, ns_ref), cast_dst, (carry,) = (refs[:n_cast], refs[n_cast:n_cast + 2],
                                                     refs[n_cast + 2:2 * n_cast + 2], refs[2 * n_cast + 2:])
    _run_casts(cast_src, cast_dst)
    tm = x_ref.shape[1]

    @pl.when(pl.program_id(1) == 0)
    def _():
        carry[0:CONV_W - 1, :] = st_ref[0, 0]

    h = x_ref[0]
    hn = _rms(h, g_ref[...]).astype(BF16)
    proj = _dot(hn, win_ref[...])
    b_g, c_g, hh = proj[:, :D_MODEL], proj[:, D_MODEL:2 * D_MODEL], proj[:, 2 * D_MODEL:]
    u = c_g * hh
    prev0, prev1 = carry[0:1, :], carry[1:2, :]
    row = lax.broadcasted_iota(jnp.int32, (tm, D_MODEL), 0)
    r1 = pltpu.roll(u, 1, 0)
    r2 = pltpu.roll(u, 2, 0)
    u1 = jnp.where(row == 0, prev1, r1)
    u2 = jnp.where(row == 0, prev0, jnp.where(row == 1, prev1, r2))
    cw = cw_ref[...]
    conv = cw[0:1, :] * u2 + cw[1:2, :] * u1 + cw[2:3, :] * u
    o_ref[0] = h + _dot((b_g * conv).astype(BF16), wout_ref[...])
    tail = r2[0:CONV_W - 1, :]
    carry[0:CONV_W - 1, :] = tail
    ns_ref[0, 0] = tail


def _conv_streams_kernel(x_ref, st_ref, g_ref, win_ref, cw_ref, wout_ref, o_ref, ns_ref, *, s):
    n = x_ref.shape[0]
    b = n // s
    h = x_ref[...]
    hn = _rms(h, g_ref[...]).astype(BF16)
    proj = _dot(hn, win_ref[...])
    b_g, c_g, hh = proj[:, :D_MODEL], proj[:, D_MODEL:2 * D_MODEL], proj[:, 2 * D_MODEL:]
    u = c_g * hh
    st = st_ref[...]
    per_row = lambda rows: jnp.broadcast_to(rows, (b, s, D_MODEL)).reshape(n, D_MODEL)
    prev0, prev1 = per_row(st[:, 0:1, :]), per_row(st[:, 1:2, :])
    pos = lax.broadcasted_iota(jnp.int32, (n, D_MODEL), 0) & (s - 1)
    u1 = jnp.where(pos == 0, prev1, pltpu.roll(u, 1, 0))
    u2 = jnp.where(pos == 0, prev0, jnp.where(pos == 1, prev1, pltpu.roll(u, 2, 0)))
    cw = cw_ref[...]
    conv = cw[0:1, :] * u2 + cw[1:2, :] * u1 + cw[2:3, :] * u
    o_ref[...] = h + _dot((b_g * conv).astype(BF16), wout_ref[...])
    ns_ref[...] = pltpu.roll(u, n - (s - (CONV_W - 1)), 0).reshape(b, s, D_MODEL)[:, 0:CONV_W - 1, :]


def _conv_streams(x, state, gain, w_in, conv_w, w_out):
    b, s, _ = x.shape
    assert s & (s - 1) == 0 and s % 8 == 0, s
    n = b * s
    full = lambda shape: pl.BlockSpec(shape, lambda i: (0,) * len(shape))
    y, ns = pl.pallas_call(
        functools.partial(_conv_streams_kernel, s=s),
        grid=(1,),
        in_specs=[full((n, D_MODEL)), full((b, CONV_W - 1, D_MODEL)),
                  full(gain.shape), full(w_in.shape), full(conv_w.shape), full(w_out.shape)],
        out_specs=[full((n, D_MODEL)), full((b, CONV_W - 1, D_MODEL))],
        out_shape=[jax.ShapeDtypeStruct((n, D_MODEL), F32), jax.ShapeDtypeStruct((b, CONV_W - 1, D_MODEL), F32)],
        compiler_params=pltpu.CompilerParams(dimension_semantics=("arbitrary",), vmem_limit_bytes=VMEM_LIMIT),
        name="conv_streams",
    )(x.reshape(n, D_MODEL), state.reshape(b, CONV_W - 1, D_MODEL), gain, w_in, conv_w, w_out)
    return y.reshape(b, s, D_MODEL), ns.reshape(b, 1, CONV_W - 1, D_MODEL)


def _conv(x, state, gain, w_in, conv_w, w_out, *, tm, casts=()):
    b, s, _ = x.shape
    tm = min(tm, s)
    nsb = s // tm
    c_args, c_in, c_shapes, c_out = _cast_jobs(casts, b * nsb, lambda bi, i: bi * nsb + i)
    return pl.pallas_call(
        functools.partial(_conv_kernel, n_cast=len(casts)),
        grid=(b, nsb),
        in_specs=[
            pl.BlockSpec((1, tm, D_MODEL), lambda bi, i: (bi, i, 0)),
            pl.BlockSpec((1, 1, CONV_W - 1, D_MODEL), lambda bi, i: (bi, 0, 0, 0)),
            _resident(gain.shape), _resident(w_in.shape), _resident(conv_w.shape), _resident(w_out.shape),
        ] + c_in,
        out_specs=[
            pl.BlockSpec((1, tm, D_MODEL), lambda bi, i: (bi, i, 0)),
            pl.BlockSpec((1, 1, CONV_W - 1, D_MODEL), lambda bi, i: (bi, 0, 0, 0)),
        ] + c_out,
        out_shape=[jax.ShapeDtypeStruct((b, s, D_MODEL), F32),
                   jax.ShapeDtypeStruct((b, 1, CONV_W - 1, D_MODEL), F32)] + c_shapes,
        scratch_shapes=[pltpu.VMEM((8, D_MODEL), F32)],
        compiler_params=pltpu.CompilerParams(dimension_semantics=("arbitrary", "arbitrary"),
                                             vmem_limit_bytes=VMEM_LIMIT),
        name="conv",
    )(x, state, gain, w_in, conv_w, w_out, *c_args)


def _stack_q(q):
    lane = lax.broadcasted_iota(jnp.int32, q.shape, 1)
    zero = jnp.zeros_like(q)
    return jnp.concatenate([jnp.where(lane < HEAD_DIM, q, zero), jnp.where(lane >= HEAD_DIM, q, zero)], axis=0)


def _chunk_of(pos):
    return lax.shift_right_logical(pos, CHUNK.bit_length() - 1)


def _lambda(lq1, lk1, lq2, lk2):
    return (jnp.exp(jnp.sum(lq1 * lk1, axis=-1, keepdims=True))
            - jnp.exp(jnp.sum(lq2 * lk2, axis=-1, keepdims=True)) + LAMBDA_INIT)


def _finish(acc, l, t, lam, subln):
    o = acc[:t] / l[:t] - lam * (acc[t:] / l[t:])
    return (_rms(o, subln) * (1.0 - LAMBDA_INIT)).astype(BF16)


def _softmax_init(m_ref, l_ref, acc_ref):
    m_ref[...] = jnp.full(m_ref.shape, NEG, F32)
    l_ref[...] = jnp.zeros(l_ref.shape, F32)
    acc_ref[...] = jnp.zeros(acc_ref.shape, F32)


def _softmax_update(s, v, m_ref, l_ref, acc_ref):
    cols = [s[:, c * LANES:(c + 1) * LANES] for c in range(s.shape[1] // LANES)]
    m_prev = m_ref[...]
    m_next = jnp.maximum(m_prev, jnp.max(functools.reduce(jnp.maximum, cols), axis=1, keepdims=True))
    alpha = jnp.exp2(m_prev - m_next)
    ps = [jnp.exp2(c - m_next) for c in cols]
    l_ref[...] = alpha * l_ref[...] + functools.reduce(jnp.add, ps)
    p = jnp.concatenate([c.astype(BF16) for c in ps], axis=1)
    acc_ref[...] = alpha * acc_ref[...] + _dot(p, v)
    m_ref[...] = m_next


ONES_ROWS = 16


def _softmax_update_t(s, cmax, vt, m_ref, acc_ref):
    m_prev = m_ref[...]
    m_next = jnp.maximum(m_prev, cmax)
    alpha = jnp.exp2(m_prev - m_next)
    p = jnp.exp2(s - m_next).astype(BF16)
    vt1 = jnp.concatenate([vt, jnp.ones((ONES_ROWS, vt.shape[1]), BF16)], axis=0)
    acc_ref[...] = alpha * acc_ref[...] + _dot(vt1, p)
    m_ref[...] = m_next


SLOT_A, SLOT_B = 0, 1
FULL_UNROLLS = (8, 4)
DIAG_UNROLLS = (4, 2)


def _attn_prompt_kernel(fi_ref, fj_ref, qt_ref, k_ref, vt_ref, sub_ref, lq1_ref, lk1_ref, lq2_ref, lk2_ref, o_ref,
                        qs_sc, s_sc, cm_sc, m_sc, acc_sc, *, t):
    nq = qt_ref.shape[2] // t
    nfull = nq * (nq - 1) // 2
    feat = lax.broadcasted_iota(jnp.int32, (V_DIM, t), 0)
    for i in range(nq):
        qt = qt_ref[0, :, i * t:(i + 1) * t]
        zero = jnp.zeros_like(qt)
        qs_sc[i] = jnp.concatenate([jnp.where(feat < HEAD_DIM, qt, zero), jnp.where(feat >= HEAD_DIM, qt, zero)],
                                   axis=1)
    m_sc[...] = jnp.full(m_sc.shape, NEG, F32)
    acc_sc[...] = jnp.zeros(acc_sc.shape, F32)

    def S(i, j, slot, masked):
        kj = k_ref[0, pl.ds(pl.multiple_of(j * t, t), t), :]
        s = _dot(kj, qs_sc[i])
        if masked:
            key = lax.broadcasted_iota(jnp.int32, s.shape, 0)
            qry = lax.broadcasted_iota(jnp.int32, s.shape, 1)
            s = jnp.where(_chunk_of(key) <= _chunk_of(qry & (t - 1)), s, NEG)
        s_sc[slot] = s
        cm_sc[slot] = jnp.max(s, axis=0, keepdims=True)

    def C(i, j, slot):
        vtj = vt_ref[0, :, pl.ds(pl.multiple_of(j * t, t), t)]
        _softmax_update_t(s_sc[slot], cm_sc[slot], vtj, m_sc.at[i], acc_sc.at[i])

    s_full = lambda n, slot: S(fi_ref[n], fj_ref[n], slot, False)
    c_full = lambda n, slot: C(fi_ref[n], fj_ref[n], slot)
    s_diag = lambda d, slot: S(d, d, slot, True)
    c_diag = lambda d, slot: C(d, d, slot)

    if nfull:
        s_full(0, SLOT_A)
    else:
        s_diag(0, SLOT_A)
    slots = (SLOT_A, SLOT_B)
    done = 0
    for unroll in FULL_UNROLLS:
        trips = max(nfull - 1 - done, 0) // unroll

        def body(it, c, unroll=unroll, base=done):
            n = base + unroll * it
            for r in range(unroll):
                s_full(n + r + 1, slots[(r + 1) % 2])
                c_full(n + r, slots[r % 2])
            return c

        lax.fori_loop(0, trips, body, 0)
        done += unroll * trips
    for r, n in enumerate(range(done, nfull)):
        cur, nxt = slots[r % 2], slots[(r + 1) % 2]
        if n + 1 < nfull:
            s_full(n + 1, nxt)
        else:
            s_diag(0, nxt)
        c_full(n, cur)
    first, second = slots[(nfull - done) % 2], slots[(nfull - done + 1) % 2]

    slots = (first, second)
    done = 0
    for unroll in DIAG_UNROLLS:
        trips = max(nq - 1 - done, 0) // unroll

        def body(it, c, unroll=unroll, base=done):
            d = base + unroll * it
            for r in range(unroll):
                s_diag(d + r + 1, slots[(r + 1) % 2])
                c_diag(d + r, slots[r % 2])
            return c

        lax.fori_loop(0, trips, body, 0)
        done += unroll * trips
    for r, d in enumerate(range(done, nq)):
        if d + 1 < nq:
            s_diag(d + 1, slots[(r + 1) % 2])
        c_diag(d, slots[r % 2])

    lam = _lambda(lq1_ref[...], lk1_ref[...], lq2_ref[...], lk2_ref[...])
    gain = sub_ref[...] * (1.0 - LAMBDA_INIT)

    def finish(i, c):
        acc = acc_sc[i]
        on = acc[0:V_DIM, :] * (1.0 / acc[V_DIM:V_DIM + 1, :])
        o = on[:, :t] - lam * on[:, t:]
        inv = lax.rsqrt(jnp.mean(o * o, axis=0, keepdims=True) + EPS)
        o_ref[0, :, pl.ds(pl.multiple_of(i * t, t), t)] = (o * inv * gain).astype(BF16)
        return c

    lax.fori_loop(0, nq, finish, 0)


def _attn_prompt(qt, k, vt, subln, lq1, lk1, lq2, lk2, *, t):
    b, s, _ = k.shape
    nq = s // t
    pairs = [(i, j) for i in range(1, nq) for j in range(i)] or [(0, 0)]
    fi = jnp.asarray([p[0] for p in pairs], jnp.int32)
    fj = jnp.asarray([p[1] for p in pairs], jnp.int32)
    small = lambda a: pl.BlockSpec(a.shape, lambda bi, h, *_: (0,) * a.ndim, pipeline_mode=pl.Buffered(1))
    subln = subln.reshape(V_DIM, 1)
    rows = pl.BlockSpec((1, s, V_DIM), lambda bi, h, *_: (bi, 0, h))
    cols = pl.BlockSpec((1, V_DIM, s), lambda bi, h, *_: (bi, h, 0))
    return pl.pallas_call(
        functools.partial(_attn_prompt_kernel, t=t),
        grid_spec=pltpu.PrefetchScalarGridSpec(
            num_scalar_prefetch=2,
            grid=(b, N_HEADS),
            in_specs=[cols, rows, cols, small(subln), small(lq1), small(lk1), small(lq2), small(lk2)],
            out_specs=cols,
            scratch_shapes=[pltpu.VMEM((nq, V_DIM, 2 * t), BF16), pltpu.VMEM((2, t, 2 * t), F32),
                            pltpu.VMEM((2, 1, 2 * t), F32), pltpu.VMEM((nq, 1, 2 * t), F32),
                            pltpu.VMEM((nq, V_DIM + ONES_ROWS, 2 * t), F32)],
        ),
        out_shape=jax.ShapeDtypeStruct((b, D_MODEL, s), BF16),
        compiler_params=pltpu.CompilerParams(dimension_semantics=("arbitrary",) * 2, vmem_limit_bytes=VMEM_LIMIT),
        name="attn_prompt",
    )(fi, fj, qt, k, vt, subln, lq1, lk1, lq2, lk2)


def _attn_sample_kernel(q_ref, kc_ref, vc_ref, kn_ref, vn_ref, sub_ref, lq1_ref, lk1_ref, lq2_ref, lk2_ref,
                        o_ref, m_sc, l_sc, acc_sc, *, q_start):
    t = q_ref.shape[1]
    tkb = kc_ref.shape[1] // N_HEADS
    j = pl.program_id(1)

    @pl.when(j == 0)
    def _():
        _softmax_init(m_sc, l_sc, acc_sc)

    head_cols = lambda h: slice(h * V_DIM, (h + 1) * V_DIM)
    scores = []
    for h in range(N_HEADS):
        qs = _stack_q(q_ref[0, :, head_cols(h)])
        kh = kc_ref[0, pl.ds(h, tkb, stride=N_HEADS), :].astype(BF16)
        scores.append(_dot_nt(qs, kh))
    for h in range(N_HEADS):
        vh = vc_ref[0, pl.ds(h, tkb, stride=N_HEADS), :].astype(BF16)
        _softmax_update(scores[h], vh, m_sc.at[h], l_sc.at[h], acc_sc.at[h])

    @pl.when(j == pl.num_programs(1) - 1)
    def _():
        lam = _lambda(lq1_ref[...], lk1_ref[...], lq2_ref[...], lk2_ref[...])
        for h in range(N_HEADS):
            qs = _stack_q(q_ref[0, :, head_cols(h)])
            s_n = _dot_nt(qs, kn_ref[0, :, head_cols(h)])
            row = lax.broadcasted_iota(jnp.int32, s_n.shape, 0)
            col = lax.broadcasted_iota(jnp.int32, s_n.shape, 1)
            s_n = jnp.where(_chunk_of(q_start + col) <= _chunk_of(q_start + (row & (t - 1))), s_n, NEG)
            m_prev = m_sc[h]
            m_fin = jnp.maximum(m_prev, jnp.max(s_n, axis=1, keepdims=True))
            alpha = jnp.exp2(m_prev - m_fin)
            p_n = jnp.exp2(s_n - m_fin[:, :t])
            l = jnp.sum(alpha * l_sc[h], axis=1, keepdims=True) + jnp.sum(p_n, axis=1, keepdims=True)
            acc = alpha * acc_sc[h] + _dot(p_n.astype(BF16), vn_ref[0, :, head_cols(h)])
            o_ref[0, :, head_cols(h)] = _finish(acc, l, t, lam, sub_ref[...])


def _attn_sample(q, cache_k, cache_v, kn, vn, subln, lq1, lk1, lq2, lk2, *, tkb):
    b, t, _ = q.shape
    past = cache_k.shape[1]
    kc = cache_k.reshape(b, past * N_HEADS, V_DIM)
    vc = cache_v.reshape(b, past * N_HEADS, V_DIM)
    small = lambda a: _resident(a.shape)
    rows = pl.BlockSpec((1, t, D_MODEL), lambda bi, j: (bi, 0, 0))
    blk = pl.BlockSpec((1, tkb * N_HEADS, V_DIM), lambda bi, j: (bi, j, 0))
    return pl.pallas_call(
        functools.partial(_attn_sample_kernel, q_start=past),
        grid=(b, past // tkb),
        in_specs=[rows, blk, blk, rows, rows, small(subln), small(lq1), small(lk1), small(lq2), small(lk2)],
        out_specs=rows,
        out_shape=jax.ShapeDtypeStruct((b, t, D_MODEL), BF16),
        scratch_shapes=[pltpu.VMEM((N_HEADS, 2 * t, LANES), F32), pltpu.VMEM((N_HEADS, 2 * t, LANES), F32),
                        pltpu.VMEM((N_HEADS, 2 * t, V_DIM), F32)],
        compiler_params=pltpu.CompilerParams(dimension_semantics=("arbitrary",) * 2, vmem_limit_bytes=VMEM_LIMIT),
        name="attn_sample",
    )(q, kc, vc, kn, vn, subln, lq1, lk1, lq2, lk2)


def _trunk(x, conv_state, cache, w, bf, *, tm_ffn, tm_conv, t_attn):
    b, s, _ = x.shape
    n = b * s

    def with_casts(call, n_out, *names):
        todo = [name for name in names if name not in bf]
        outs = call([w["f32"][name] for name in todo])
        bf.update(zip(todo, outs[n_out:]))
        return outs[:n_out]

    (h,) = with_casts(lambda casts: _ffn(x.reshape(n, D_MODEL), w["g_ffn1"][0], bf["ffn_in00"], bf["ffn_out00"],
                                         tm=tm_ffn, casts=casts),
                      1, "conv_in", "conv_out", "ffn_in01", "ffn_out01")
    conv_args = (h.reshape(b, s, D_MODEL), conv_state, w["g_mix"][0], bf["conv_in"], w["conv_w"], bf["conv_out"])
    if s < tm_conv:
        h, new_conv = _conv_streams(*conv_args)
    else:
        h, new_conv = with_casts(lambda casts: _conv(*conv_args, tm=tm_conv, casts=casts),
                                 2, "w_q", "ffn_in10", "ffn_out10")
    x1, k_new, v_new, k_bf, v_bf = with_casts(
        lambda casts: _ffn(h.reshape(n, D_MODEL), w["g_ffn2"][0], bf["ffn_in01"], bf["ffn_out01"], tm=tm_ffn,
                           proj="kvt" if cache is None else "kv", proj_gain=w["g_kv"], proj_w=bf["w_kv"], seq=s,
                           casts=casts),
        5, "ffn_in11", "ffn_out11")
    h1, q_bf = with_casts(
        lambda casts: _ffn(x1, w["g_ffn1"][1], bf["ffn_in10"], bf["ffn_out10"], tm=tm_ffn,
                           proj="qt" if cache is None else "q", proj_gain=w["g_mix"][1], proj_w=bf["w_q"], seq=s,
                           casts=casts),
        2, "w_o")
    shp = (b, s, D_MODEL)
    lam = (w["lq1"], w["lk1"], w["lq2"], w["lk2"])
    if cache is None:
        o = _attn_prompt(q_bf, k_bf.reshape(shp), v_bf, w["subln"], *lam, t=t_attn)
    else:
        o = _attn_sample(q_bf.reshape(shp), cache[0], cache[1], k_bf.reshape(shp), v_bf.reshape(shp),
                         w["subln"], *lam, tkb=min(2048, cache[0].shape[1])).reshape(n, D_MODEL)
    (y,) = _ffn(h1, w["g_ffn2"][1], bf["ffn_in11"], bf["ffn_out11"], tm=tm_ffn, pre=(o, bf["w_o"]),
                final_gain=w["g_final"])
    kv_shape = (b, s, N_HEADS, V_DIM)
    return y.reshape(shp), new_conv, k_new.reshape(kv_shape), v_new.reshape(kv_shape)


def kernel(x_prompt, x_sample, state_conv, cache_k, cache_v, norm_ffn1, norm_mix, norm_ffn2, ffn_w_in, ffn_w_out,
           conv_w_in, conv_w, conv_w_out, norm_kv, w_k, w_v, w_q, lambda_q1, lambda_k1, lambda_q2, lambda_k2,
           subln, w_o, norm_final):
    depth = norm_ffn1.shape[0]
    row = lambda a: a.reshape(1, -1)
    w = {
        "g_ffn1": [row(norm_ffn1[i]) for i in range(depth)],
        "g_mix": [row(norm_mix[i]) for i in range(depth)],
        "g_ffn2": [row(norm_ffn2[i]) for i in range(depth)],
        "conv_w": conv_w[0],
        "g_kv": row(norm_kv),
        "lq1": row(lambda_q1[0]), "lk1": row(lambda_k1[0]), "lq2": row(lambda_q2[0]), "lk2": row(lambda_k2[0]),
        "subln": row(subln[0]),
        "g_final": row(norm_final),
        "f32": {
            "conv_in": (conv_w_in, (0,)), "conv_out": (conv_w_out, (0,)), "w_q": (w_q, (0,)), "w_o": (w_o, (0,)),
            "ffn_in01": (ffn_w_in, (0, 1)), "ffn_out01": (ffn_w_out, (0, 1)),
            "ffn_in10": (ffn_w_in, (1, 0)), "ffn_out10": (ffn_w_out, (1, 0)),
            "ffn_in11": (ffn_w_in, (1, 1)), "ffn_out11": (ffn_w_out, (1, 1)),
        },
    }
    bf = {"ffn_in00": ffn_w_in[0, 0].astype(BF16), "ffn_out00": ffn_w_out[0, 0].astype(BF16),
          "w_kv": jnp.concatenate([w_k, w_v], axis=1).astype(BF16)}
    b = x_prompt.shape[0]
    conv0 = jnp.zeros((b, 1, CONV_W - 1, D_MODEL), x_prompt.dtype)
    y_p, conv_p, k_p, v_p = _trunk(x_prompt, conv0, None, w, bf, tm_ffn=512, tm_conv=512, t_attn=512)
    y_s, conv_s, k_s, v_s = _trunk(x_sample, state_conv, (cache_k, cache_v), w, bf, tm_ffn=512, tm_conv=512,
                                   t_attn=None)
    return (y_p, y_s, conv_p, k_p, v_p, conv_s, k_s, v_s)
```

```python
import functools
import math

import jax
import jax.numpy as jnp
from jax import lax
from jax.experimental import pallas as pl
from jax.experimental.pallas import tpu as pltpu

D_MODEL = 1024
CHUNK = 64
CONV_W = 3
D_FF = 2816
N_HEADS = 8
HEAD_DIM = 64
V_DIM = 2 * HEAD_DIM
EPS = 1e-6
NEG = -1e30
LAMBDA_INIT = 0.8 - 0.6 * math.exp(-0.3 * 1)
LOG2E = math.log2(math.e)
Q_SCALE = HEAD_DIM ** -0.5 * LOG2E

V7X_VMEM_BYTES = 64 * 1024 * 1024
VMEM_LIMIT = V7X_VMEM_BYTES - 8 * 1024 * 1024
LANES = 128

BF16 = jnp.bfloat16
F32 = jnp.float32


def _resident(shape):
    nd = len(shape)
    return pl.BlockSpec(shape, lambda *_: (0,) * nd, pipeline_mode=pl.Buffered(1))


BF16_SUBLANES = 16


def _cast_jobs(jobs, nsteps, step_of):
    args, in_specs, out_shapes, out_specs = [], [], [], []
    for array, lead in jobs:
        rows, cols = array.shape[len(lead):]
        nblk = nsteps
        while rows % nblk or (rows // nblk) % BF16_SUBLANES:
            assert nblk % 2 == 0, (array.shape, nsteps)
            nblk //= 2
        rep, br = nsteps // nblk, rows // nblk
        args.append(array)
        in_specs.append(pl.BlockSpec((None,) * len(lead) + (br, cols),
                                     lambda *ids, lead=tuple(lead), rep=rep: lead + (step_of(*ids) // rep, 0)))
        out_shapes.append(jax.ShapeDtypeStruct((rows, cols), BF16))
        out_specs.append(pl.BlockSpec((br, cols), lambda *ids, rep=rep: (step_of(*ids) // rep, 0)))
    return args, in_specs, out_shapes, out_specs


def _run_casts(src_refs, dst_refs):
    for src, dst in zip(src_refs, dst_refs):
        dst[...] = src[...].astype(BF16)


def _rms(x, g):
    return x * lax.rsqrt(jnp.mean(x * x, axis=-1, keepdims=True) + EPS) * g


def _dot(a, b):
    return jnp.dot(a, b, preferred_element_type=F32)


def _dot_nt(a, b):
    return lax.dot_general(a, b, (((1,), (1,)), ((), ())), preferred_element_type=F32)


def _ffn_kernel(*refs, pre_proj, final_norm, proj, n_cast):
    refs = list(refs)
    x_ref = refs.pop(0)
    if pre_proj:
        a_ref, wpre_ref = refs.pop(0), refs.pop(0)
    g_ref, win_ref, wout_ref = refs.pop(0), refs.pop(0), refs.pop(0)
    if final_norm:
        gf_ref = refs.pop(0)
    if proj is not None:
        gp_ref, wp_ref = refs.pop(0), refs.pop(0)
    cast_src = [refs.pop(0) for _ in range(n_cast)]
    y_ref = refs.pop(0)
    cast_dst = [refs.pop() for _ in range(n_cast)][::-1]
    _run_casts(cast_src, cast_dst)

    x = x_ref[...]
    if pre_proj == "rows":
        x = x + _dot(a_ref[...], wpre_ref[...])
    elif pre_proj == "cols":
        x = x + lax.dot_general(a_ref[0], wpre_ref[...], (((0,), (0,)), ((), ())), preferred_element_type=F32)
    xn = _rms(x, g_ref[...]).astype(BF16)
    gu = _dot(xn, win_ref[...])
    g, u = gu[:, :D_FF], gu[:, D_FF:]
    act = (g * (1.0 / (1.0 + jnp.exp(-g))) * u).astype(BF16)
    y = x + 0.5 * _dot(act, wout_ref[...])
    _ffn_outputs(y, y_ref, gf_ref if final_norm else None, proj, (gp_ref, wp_ref) if proj else None, refs)


def _ffn_outputs(y, y_ref, gf_ref, proj, proj_refs, out_refs):
    y_ref[...] = y if gf_ref is None else _rms(y, gf_ref[...])
    if proj is None:
        return
    gp_ref, wp_ref = proj_refs
    r = _dot(_rms(y, gp_ref[...]).astype(BF16), wp_ref[...])
    if proj in ("kv", "kvt"):
        k, v = r[:, :D_MODEL], r[:, D_MODEL:]
        kf_ref, vf_ref, kb_ref, vb_ref = out_refs
        if proj == "kv":
            vb_ref[...] = v.astype(BF16)
        else:
            vb_ref[0] = v.T.astype(BF16)
        kf_ref[...] = k
        vf_ref[...] = v
        kb_ref[...] = k.astype(BF16)
    else:
        (qb_ref,) = out_refs
        if proj == "q":
            qb_ref[...] = (r * Q_SCALE).astype(BF16)
        else:
            qb_ref[0] = (r * Q_SCALE).T.astype(BF16)


FF_CHUNK = 256


def _ffn_chunked_kernel(*refs, pre, final_norm, proj):
    refs = list(refs)
    take = lambda k: [refs.pop(0) for _ in range(k)]
    (x_ref,) = take(1)
    if pre:
        a_ref, wpre_ref = take(2)
    g_ref, wg_ref, wu_ref, wo_ref = take(4)
    gf_ref = take(1)[0] if final_norm else None
    proj_refs = take(2) if proj else None
    (y_ref,) = take(1)
    acc_sc, xn_sc = refs.pop(), refs.pop()
    x_sc = refs.pop() if pre else None
    c = pl.program_id(0)

    @pl.when(c == 0)
    def _():
        x = x_ref[...]
        if pre:
            x = x + _dot(a_ref[...], wpre_ref[...])
            x_sc[...] = x
        xn_sc[...] = _rms(x, g_ref[...]).astype(BF16)
        acc_sc[...] = jnp.zeros(acc_sc.shape, F32)

    xn = xn_sc[...]
    g, u = _dot(xn, wg_ref[...]), _dot(xn, wu_ref[...])
    act = (g * (1.0 / (1.0 + jnp.exp(-g))) * u).astype(BF16)
    acc_sc[...] += _dot(act, wo_ref[...])

    @pl.when(c == pl.num_programs(0) - 1)
    def _():
        x = x_sc[...] if pre else x_ref[...]
        _ffn_outputs(x + 0.5 * acc_sc[...], y_ref, gf_ref, proj, proj_refs, refs)


def _ffn_chunked(x, gain, w_in, w_out, *, pre=None, final_gain=None, proj=None, proj_gain=None, proj_w=None):
    n = x.shape[0]
    nc = D_FF // FF_CHUNK
    assert D_FF == nc * FF_CHUNK and proj in (None, "q", "kv")
    whole = lambda a: pl.BlockSpec(a.shape, lambda c: (0,) * a.ndim, pipeline_mode=pl.Buffered(1))
    args, specs = [x], [whole(x)]
    scratch = []
    if pre is not None:
        args += list(pre)
        specs += [whole(pre[0]), whole(pre[1])]
        scratch.append(pltpu.VMEM((n, D_MODEL), F32))
    args += [gain, w_in, w_in, w_out]
    specs += [whole(gain),
              pl.BlockSpec((D_MODEL, FF_CHUNK), lambda c: (0, c)),
              pl.BlockSpec((D_MODEL, FF_CHUNK), lambda c: (0, nc + c)),
              pl.BlockSpec((FF_CHUNK, D_MODEL), lambda c: (c, 0))]
    if final_gain is not None:
        args.append(final_gain)
        specs.append(whole(final_gain))
    if proj is not None:
        args += [proj_gain, proj_w]
        specs += [whole(proj_gain), whole(proj_w)]
    dts = (F32,) + {None: (), "kv": (F32, F32, BF16, BF16), "q": (BF16,)}[proj]
    scratch += [pltpu.VMEM((n, D_MODEL), BF16), pltpu.VMEM((n, D_MODEL), F32)]
    return pl.pallas_call(
        functools.partial(_ffn_chunked_kernel, pre=pre is not None, final_norm=final_gain is not None, proj=proj),
        grid=(nc,),
        in_specs=specs,
        out_specs=[pl.BlockSpec((n, D_MODEL), lambda c: (0, 0)) for _ in dts],
        out_shape=[jax.ShapeDtypeStruct((n, D_MODEL), dt) for dt in dts],
        scratch_shapes=scratch,
        compiler_params=pltpu.CompilerParams(dimension_semantics=("arbitrary",), vmem_limit_bytes=VMEM_LIMIT),
        name="ffn1blk" + ("_pre" if pre is not None else "") + ("_" + proj if proj else "") + ("_fin" if final_gain is not None else ""),
    )(*args)


def _ffn(x, gain, w_in, w_out, *, tm, pre=None, final_gain=None, proj=None, proj_gain=None, proj_w=None,
         seq=None, casts=()):
    n = x.shape[0]
    if n <= tm:
        assert not casts
        return _ffn_chunked(x, gain, w_in, w_out, pre=pre, final_gain=final_gain, proj=proj, proj_gain=proj_gain,
                            proj_w=proj_w)
    row = lambda cols: pl.BlockSpec((tm, cols), lambda i: (i, 0))
    args, specs = [x], [row(D_MODEL)]
    pre_kind = None
    if pre is not None:
        a, w_pre = pre
        args += [a, w_pre]
        if a.ndim == 2:
            pre_kind = "rows"
            specs += [row(D_MODEL), _resident(w_pre.shape)]
        else:
            pre_kind = "cols"
            nsb_a = a.shape[2] // tm
            specs += [pl.BlockSpec((1, D_MODEL, tm), lambda i: (i // nsb_a, 0, i % nsb_a)), _resident(w_pre.shape)]
    args += [gain, w_in, w_out]
    specs += [_resident(gain.shape), _resident(w_in.shape), _resident(w_out.shape)]
    if final_gain is not None:
        args.append(final_gain)
        specs.append(_resident(final_gain.shape))
    out_shape = [jax.ShapeDtypeStruct((n, D_MODEL), F32)]
    out_specs = [row(D_MODEL)]
    if proj is not None:
        args += [proj_gain, proj_w]
        specs += [_resident(proj_gain.shape), _resident(proj_w.shape)]
        dts = {"kv": (F32, F32, BF16, BF16), "kvt": (F32, F32, BF16), "q": (BF16,), "qt": ()}[proj]
        out_shape += [jax.ShapeDtypeStruct((n, D_MODEL), dt) for dt in dts]
        out_specs += [row(D_MODEL) for _ in dts]
        if proj in ("kvt", "qt"):
            nsb = seq // tm
            out_shape.append(jax.ShapeDtypeStruct((n // seq, D_MODEL, seq), BF16))
            out_specs.append(pl.BlockSpec((1, D_MODEL, tm), lambda i: (i // nsb, 0, i % nsb)))
    c_args, c_in, c_shapes, c_out = _cast_jobs(casts, n // tm, lambda i: i)
    args += c_args
    specs += c_in
    out_shape += c_shapes
    out_specs += c_out
    return pl.pallas_call(
        functools.partial(_ffn_kernel, pre_proj=pre_kind, final_norm=final_gain is not None, proj=proj,
                          n_cast=len(casts)),
        grid=(n // tm,),
        in_specs=specs,
        out_specs=out_specs,
        out_shape=out_shape,
        compiler_params=pltpu.CompilerParams(dimension_semantics=("arbitrary",), vmem_limit_bytes=VMEM_LIMIT),
        name="ffn" + ("_pre" if pre is not None else "") + ("_" + proj if proj else "") + ("_fin" if final_gain is not None else ""),
    )(*args)


def _conv_kernel(x_ref, st_ref, g_ref, win_ref, cw_ref, wout_ref, *refs, n_cast):
    cast_src, (o_ref, ns_ref), cast_dst, (carry,) = (refs[:n_cast], refs[n_cast:n_cast + 2],
                                                     refs[n_cast + 2:2 * n_cast + 2], refs[2 * n_cast + 2:])
    _run_casts(cast_src, cast_dst)
    tm = x_ref.shape[1]

    @pl.when(pl.program_id(1) == 0)
    def _():
        carry[0:CONV_W - 1, :] = st_ref[0, 0]

    h = x_ref[0]
    hn = _rms(h, g_ref[...]).astype(BF16)
    proj = _dot(hn, win_ref[...])
    b_g, c_g, hh = proj[:, :D_MODEL], proj[:, D_MODEL:2 * D_MODEL], proj[:, 2 * D_MODEL:]
    u = c_g * hh
    prev0, prev1 = carry[0:1, :], carry[1:2, :]
    row = lax.broadcasted_iota(jnp.int32, (tm, D_MODEL), 0)
    r1 = pltpu.roll(u, 1, 0)
    r2 = pltpu.roll(u, 2, 0)
    u1 = jnp.where(row == 0, prev1, r1)
    u2 = jnp.where(row == 0, prev0, jnp.where(row == 1, prev1, r2))
    cw = cw_ref[...]
    conv = cw[0:1, :] * u2 + cw[1:2, :] * u1 + cw[2:3, :] * u
    o_ref[0] = h + _dot((b_g * conv).astype(BF16), wout_ref[...])
    tail = r2[0:CONV_W - 1, :]
    carry[0:CONV_W - 1, :] = tail
    ns_ref[0, 0] = tail


def _conv_streams_kernel(x_ref, st_ref, g_ref, win_ref, cw_ref, wout_ref, o_ref, ns_ref, *, s):
    n = x_ref.shape[0]
    b = n // s
    h = x_ref[...]
    hn = _rms(h, g_ref[...]).astype(BF16)
    proj = _dot(hn, win_ref[...])
    b_g, c_g, hh = proj[:, :D_MODEL], proj[:, D_MODEL:2 * D_MODEL], proj[:, 2 * D_MODEL:]
    u = c_g * hh
    st = st_ref[...]
    per_row = lambda rows: jnp.broadcast_to(rows, (b, s, D_MODEL)).reshape(n, D_MODEL)
    prev0, prev1 = per_row(st[:, 0:1, :]), per_row(st[:, 1:2, :])
    pos = lax.broadcasted_iota(jnp.int32, (n, D_MODEL), 0) & (s - 1)
    u1 = jnp.where(pos == 0, prev1, pltpu.roll(u, 1, 0))
    u2 = jnp.where(pos == 0, prev0, jnp.where(pos == 1, prev1, pltpu.roll(u, 2, 0)))
    cw = cw_ref[...]
    conv = cw[0:1, :] * u2 + cw[1:2, :] * u1 + cw[2:3, :] * u
    o_ref[...] = h + _dot((b_g * conv).astype(BF16), wout_ref[...])
    ns_ref[...] = pltpu.roll(u, n - (s - (CONV_W - 1)), 0).reshape(b, s, D_MODEL)[:, 0:CONV_W - 1, :]


def _conv_streams(x, state, gain, w_in, conv_w, w_out):
    b, s, _ = x.shape
    assert s & (s - 1) == 0 and s % 8 == 0, s
    n = b * s
    full = lambda shape: pl.BlockSpec(shape, lambda i: (0,) * len(shape))
    y, ns = pl.pallas_call(
        functools.partial(_conv_streams_kernel, s=s),
        grid=(1,),
        in_specs=[full((n, D_MODEL)), full((b, CONV_W - 1, D_MODEL)),
                  full(gain.shape), full(w_in.shape), full(conv_w.shape), full(w_out.shape)],
        out_specs=[full((n, D_MODEL)), full((b, CONV_W - 1, D_MODEL))],
        out_shape=[jax.ShapeDtypeStruct((n, D_MODEL), F32), jax.ShapeDtypeStruct((b, CONV_W - 1, D_MODEL), F32)],
        compiler_params=pltpu.CompilerParams(dimension_semantics=("arbitrary",), vmem_limit_bytes=VMEM_LIMIT),
        name="conv_streams",
    )(x.reshape(n, D_MODEL), state.reshape(b, CONV_W - 1, D_MODEL), gain, w_in, conv_w, w_out)
    return y.reshape(b, s, D_MODEL), ns.reshape(b, 1, CONV_W - 1, D_MODEL)


def _conv(x, state, gain, w_in, conv_w, w_out, *, tm, casts=()):
    b, s, _ = x.shape
    tm = min(tm, s)
    nsb = s // tm
    c_args, c_in, c_shapes, c_out = _cast_jobs(casts, b * nsb, lambda bi, i: bi * nsb + i)
    return pl.pallas_call(
        functools.partial(_conv_kernel, n_cast=len(casts)),
        grid=(b, nsb),
        in_specs=[
            pl.BlockSpec((1, tm, D_MODEL), lambda bi, i: (bi, i, 0)),
            pl.BlockSpec((1, 1, CONV_W - 1, D_MODEL), lambda bi, i: (bi, 0, 0, 0)),
            _resident(gain.shape), _resident(w_in.shape), _resident(conv_w.shape), _resident(w_out.shape),
        ] + c_in,
        out_specs=[
            pl.BlockSpec((1, tm, D_MODEL), lambda bi, i: (bi, i, 0)),
            pl.BlockSpec((1, 1, CONV_W - 1, D_MODEL), lambda bi, i: (bi, 0, 0, 0)),
        ] + c_out,
        out_shape=[jax.ShapeDtypeStruct((b, s, D_MODEL), F32),
                   jax.ShapeDtypeStruct((b, 1, CONV_W - 1, D_MODEL), F32)] + c_shapes,
        scratch_shapes=[pltpu.VMEM((8, D_MODEL), F32)],
        compiler_params=pltpu.CompilerParams(dimension_semantics=("arbitrary", "arbitrary"),
                                             vmem_limit_bytes=VMEM_LIMIT),
        name="conv",
    )(x, state, gain, w_in, conv_w, w_out, *c_args)


def _stack_q(q):
    lane = lax.broadcasted_iota(jnp.int32, q.shape, 1)
    zero = jnp.zeros_like(q)
    return jnp.concatenate([jnp.where(lane < HEAD_DIM, q, zero), jnp.where(lane >= HEAD_DIM, q, zero)], axis=0)


def _chunk_of(pos):
    return lax.shift_right_logical(pos, CHUNK.bit_length() - 1)


def _lambda(lq1, lk1, lq2, lk2):
    return (jnp.exp(jnp.sum(lq1 * lk1, axis=-1, keepdims=True))
            - jnp.exp(jnp.sum(lq2 * lk2, axis=-1, keepdims=True)) + LAMBDA_INIT)


def _finish(acc, l, t, lam, subln):
    o = acc[:t] / l[:t] - lam * (acc[t:] / l[t:])
    return (_rms(o, subln) * (1.0 - LAMBDA_INIT)).astype(BF16)


def _softmax_init(m_ref, l_ref, acc_ref):
    m_ref[...] = jnp.full(m_ref.shape, NEG, F32)
    l_ref[...] = jnp.zeros(l_ref.shape, F32)
    acc_ref[...] = jnp.zeros(acc_ref.shape, F32)


def _softmax_update(s, v, m_ref, l_ref, acc_ref):
    cols = [s[:, c * LANES:(c + 1) * LANES] for c in range(s.shape[1] // LANES)]
    m_prev = m_ref[...]
    m_next = jnp.maximum(m_prev, jnp.max(functools.reduce(jnp.maximum, cols), axis=1, keepdims=True))
    alpha = jnp.exp2(m_prev - m_next)
    ps = [jnp.exp2(c - m_next) for c in cols]
    l_ref[...] = alpha * l_ref[...] + functools.reduce(jnp.add, ps)
    p = jnp.concatenate([c.astype(BF16) for c in ps], axis=1)
    acc_ref[...] = alpha * acc_ref[...] + _dot(p, v)
    m_ref[...] = m_next


ONES_ROWS = 16


def _softmax_update_t(s, cmax, vt, m_ref, acc_ref):
    m_prev = m_ref[...]
    m_next = jnp.maximum(m_prev, cmax)
    alpha = jnp.exp2(m_prev - m_next)
    p = jnp.exp2(s - m_next).astype(BF16)
    vt1 = jnp.concatenate([vt, jnp.ones((ONES_ROWS, vt.shape[1]), BF16)], axis=0)
    acc_ref[...] = alpha * acc_ref[...] + _dot(vt1, p)
    m_ref[...] = m_next


SLOT_A, SLOT_B = 0, 1
FULL_UNROLLS = (8, 4)
DIAG_UNROLLS = (4, 2)


def _attn_prompt_kernel(fi_ref, fj_ref, qt_ref, k_ref, vt_ref, sub_ref, lq1_ref, lk1_ref, lq2_ref, lk2_ref, o_ref,
                        qs_sc, s_sc, cm_sc, m_sc, acc_sc, *, t):
    nq = qt_ref.shape[2] // t
    nfull = nq * (nq - 1) // 2
    feat = lax.broadcasted_iota(jnp.int32, (V_DIM, t), 0)
    for i in range(nq):
        qt = qt_ref[0, :, i * t:(i + 1) * t]
        zero = jnp.zeros_like(qt)
        qs_sc[i] = jnp.concatenate([jnp.where(feat < HEAD_DIM, qt, zero), jnp.where(feat >= HEAD_DIM, qt, zero)],
                                   axis=1)
    m_sc[...] = jnp.full(m_sc.shape, NEG, F32)
    acc_sc[...] = jnp.zeros(acc_sc.shape, F32)

    def S(i, j, slot, masked):
        kj = k_ref[0, pl.ds(pl.multiple_of(j * t, t), t), :]
        s = _dot(kj, qs_sc[i])
        if masked:
            key = lax.broadcasted_iota(jnp.int32, s.shape, 0)
            qry = lax.broadcasted_iota(jnp.int32, s.shape, 1)
            s = jnp.where(_chunk_of(key) <= _chunk_of(qry & (t - 1)), s, NEG)
        s_sc[slot] = s
        cm_sc[slot] = jnp.max(s, axis=0, keepdims=True)

    def C(i, j, slot):
        vtj = vt_ref[0, :, pl.ds(pl.multiple_of(j * t, t), t)]
        _softmax_update_t(s_sc[slot], cm_sc[slot], vtj, m_sc.at[i], acc_sc.at[i])

    s_full = lambda n, slot: S(fi_ref[n], fj_ref[n], slot, False)
    c_full = lambda n, slot: C(fi_ref[n], fj_ref[n], slot)
    s_diag = lambda d, slot: S(d, d, slot, True)
    c_diag = lambda d, slot: C(d, d, slot)

    if nfull:
        s_full(0, SLOT_A)
    else:
        s_diag(0, SLOT_A)
    slots = (SLOT_A, SLOT_B)
    done = 0
    for unroll in FULL_UNROLLS:
        trips = max(nfull - 1 - done, 0) // unroll

        def body(it, c, unroll=unroll, base=done):
            n = base + unroll * it
            for r in range(unroll):
                s_full(n + r + 1, slots[(r + 1) % 2])
                c_full(n + r, slots[r % 2])
            return c

        lax.fori_loop(0, trips, body, 0)
        done += unroll * trips
    for r, n in enumerate(range(done, nfull)):
        cur, nxt = slots[r % 2], slots[(r + 1) % 2]
        if n + 1 < nfull:
            s_full(n + 1, nxt)
        else:
            s_diag(0, nxt)
        c_full(n, cur)
    first, second = slots[(nfull - done) % 2], slots[(nfull - done + 1) % 2]

    slots = (first, second)
    done = 0
    for unroll in DIAG_UNROLLS:
        trips = max(nq - 1 - done, 0) // unroll

        def body(it, c, unroll=unroll, base=done):
            d = base + unroll * it
            for r in range(unroll):
                s_diag(d + r + 1, slots[(r + 1) % 2])
                c_diag(d + r, slots[r % 2])
            return c

        lax.fori_loop(0, trips, body, 0)
        done += unroll * trips
    for r, d in enumerate(range(done, nq)):
        if d + 1 < nq:
            s_diag(d + 1, slots[(r + 1) % 2])
        c_diag(d, slots[r % 2])

    lam = _lambda(lq1_ref[...], lk1_ref[...], lq2_ref[...], lk2_ref[...])
    gain = sub_ref[...] * (1.0 - LAMBDA_INIT)

    def finish(i, c):
        acc = acc_sc[i]
        on = acc[0:V_DIM, :] * (1.0 / acc[V_DIM:V_DIM + 1, :])
        o = on[:, :t] - lam * on[:, t:]
        inv = lax.rsqrt(jnp.mean(o * o, axis=0, keepdims=True) + EPS)
        o_ref[0, :, pl.ds(pl.multiple_of(i * t, t), t)] = (o * inv * gain).astype(BF16)
        return c

    lax.fori_loop(0, nq, finish, 0)


def _attn_prompt(qt, k, vt, subln, lq1, lk1, lq2, lk2, *, t):
    b, s, _ = k.shape
    nq = s // t
    pairs = [(i, j) for i in range(1, nq) for j in range(i)] or [(0, 0)]
    fi = jnp.asarray([p[0] for p in pairs], jnp.int32)
    fj = jnp.asarray([p[1] for p in pairs], jnp.int32)
    small = lambda a: pl.BlockSpec(a.shape, lambda bi, h, *_: (0,) * a.ndim, pipeline_mode=pl.Buffered(1))
    subln = subln.reshape(V_DIM, 1)
    rows = pl.BlockSpec((1, s, V_DIM), lambda bi, h, *_: (bi, 0, h))
    cols = pl.BlockSpec((1, V_DIM, s), lambda bi, h, *_: (bi, h, 0))
    return pl.pallas_call(
        functools.partial(_attn_prompt_kernel, t=t),
        grid_spec=pltpu.PrefetchScalarGridSpec(
            num_scalar_prefetch=2,
            grid=(b, N_HEADS),
            in_specs=[cols, rows, cols, small(subln), small(lq1), small(lk1), small(lq2), small(lk2)],
            out_specs=cols,
            scratch_shapes=[pltpu.VMEM((nq, V_DIM, 2 * t), BF16), pltpu.VMEM((2, t, 2 * t), F32),
                            pltpu.VMEM((2, 1, 2 * t), F32), pltpu.VMEM((nq, 1, 2 * t), F32),
                            pltpu.VMEM((nq, V_DIM + ONES_ROWS, 2 * t), F32)],
        ),
        out_shape=jax.ShapeDtypeStruct((b, D_MODEL, s), BF16),
        compiler_params=pltpu.CompilerParams(dimension_semantics=("arbitrary",) * 2, vmem_limit_bytes=VMEM_LIMIT),
        name="attn_prompt",
    )(fi, fj, qt, k, vt, subln, lq1, lk1, lq2, lk2)


def _attn_sample_kernel(q_ref, kc_ref, vc_ref, kn_ref, vn_ref, sub_ref, lq1_ref, lk1_ref, lq2_ref, lk2_ref,
                        o_ref, m_sc, l_sc, acc_sc, *, q_start):
    t = q_ref.shape[1]
    tkb = kc_ref.shape[1] // N_HEADS
    j = pl.program_id(1)

    @pl.when(j == 0)
    def _():
        _softmax_init(m_sc, l_sc, acc_sc)

    head_cols = lambda h: slice(h * V_DIM, (h + 1) * V_DIM)
    scores = []
    for h in range(N_HEADS):
        qs = _stack_q(q_ref[0, :, head_cols(h)])
        kh = kc_ref[0, pl.ds(h, tkb, stride=N_HEADS), :].astype(BF16)
        scores.append(_dot_nt(qs, kh))
    for h in range(N_HEADS):
        vh = vc_ref[0, pl.ds(h, tkb, stride=N_HEADS), :].astype(BF16)
        _softmax_update(scores[h], vh, m_sc.at[h], l_sc.at[h], acc_sc.at[h])

    @pl.when(j == pl.num_programs(1) - 1)
    def _():
        lam = _lambda(lq1_ref[...], lk1_ref[...], lq2_ref[...], lk2_ref[...])
        for h in range(N_HEADS):
            qs = _stack_q(q_ref[0, :, head_cols(h)])
            s_n = _dot_nt(qs, kn_ref[0, :, head_cols(h)])
            row = lax.broadcasted_iota(jnp.int32, s_n.shape, 0)
            col = lax.broadcasted_iota(jnp.int32, s_n.shape, 1)
            s_n = jnp.where(_chunk_of(q_start + col) <= _chunk_of(q_start + (row & (t - 1))), s_n, NEG)
            m_prev = m_sc[h]
            m_fin = jnp.maximum(m_prev, jnp.max(s_n, axis=1, keepdims=True))
            alpha = jnp.exp2(m_prev - m_fin)
            p_n = jnp.exp2(s_n - m_fin[:, :t])
            l = jnp.sum(alpha * l_sc[h], axis=1, keepdims=True) + jnp.sum(p_n, axis=1, keepdims=True)
            acc = alpha * acc_sc[h] + _dot(p_n.astype(BF16), vn_ref[0, :, head_cols(h)])
            o_ref[0, :, head_cols(h)] = _finish(acc, l, t, lam, sub_ref[...])


def _attn_sample(q, cache_k, cache_v, kn, vn, subln, lq1, lk1, lq2, lk2, *, tkb):
    b, t, _ = q.shape
    past = cache_k.shape[1]
    kc = cache_k.reshape(b, past * N_HEADS, V_DIM)
    vc = cache_v.reshape(b, past * N_HEADS, V_DIM)
    small = lambda a: _resident(a.shape)
    rows = pl.BlockSpec((1, t, D_MODEL), lambda bi, j: (bi, 0, 0))
    blk = pl.BlockSpec((1, tkb * N_HEADS, V_DIM), lambda bi, j: (bi, j, 0))
    return pl.pallas_call(
        functools.partial(_attn_sample_kernel, q_start=past),
        grid=(b, past // tkb),
        in_specs=[rows, blk, blk, rows, rows, small(subln), small(lq1), small(lk1), small(lq2), small(lk2)],
        out_specs=rows,
        out_shape=jax.ShapeDtypeStruct((b, t, D_MODEL), BF16),
        scratch_shapes=[pltpu.VMEM((N_HEADS, 2 * t, LANES), F32), pltpu.VMEM((N_HEADS, 2 * t, LANES), F32),
                        pltpu.VMEM((N_HEADS, 2 * t, V_DIM), F32)],
        compiler_params=pltpu.CompilerParams(dimension_semantics=("arbitrary",) * 2, vmem_limit_bytes=VMEM_LIMIT),
        name="attn_sample",
    )(q, kc, vc, kn, vn, subln, lq1, lk1, lq2, lk2)


def _trunk(x, conv_state, cache, w, bf, *, tm_ffn, tm_conv, t_attn):
    b, s, _ = x.shape
    n = b * s

    def with_casts(call, n_out, *names):
        todo = [name for name in names if name not in bf]
        outs = call([w["f32"][name] for name in todo])
        bf.update(zip(todo, outs[n_out:]))
        return outs[:n_out]

    (h,) = with_casts(lambda casts: _ffn(x.reshape(n, D_MODEL), w["g_ffn1"][0], bf["ffn_in00"], bf["ffn_out00"],
                                         tm=tm_ffn, casts=casts),
                      1, "conv_in", "conv_out", "ffn_in01", "ffn_out01")
    conv_args = (h.reshape(b, s, D_MODEL), conv_state, w["g_mix"][0], bf["conv_in"], w["conv_w"], bf["conv_out"])
    if s < tm_conv:
        h, new_conv = _conv_streams(*conv_args)
    else:
        h, new_conv = with_casts(lambda casts: _conv(*conv_args, tm=tm_conv, casts=casts),
                                 2, "w_q", "ffn_in10", "ffn_out10")
    x1, k_new, v_new, k_bf, v_bf = with_casts(
        lambda casts: _ffn(h.reshape(n, D_MODEL), w["g_ffn2"][0], bf["ffn_in01"], bf["ffn_out01"], tm=tm_ffn,
                           proj="kvt" if cache is None else "kv", proj_gain=w["g_kv"], proj_w=bf["w_kv"], seq=s,
                           casts=casts),
        5, "ffn_in11", "ffn_out11")
    h1, q_bf = with_casts(
        lambda casts: _ffn(x1, w["g_ffn1"][1], bf["ffn_in10"], bf["ffn_out10"], tm=tm_ffn,
                           proj="qt" if cache is None else "q", proj_gain=w["g_mix"][1], proj_w=bf["w_q"], seq=s,
                           casts=casts),
        2, "w_o")
    shp = (b, s, D_MODEL)
    lam = (w["lq1"], w["lk1"], w["lq2"], w["lk2"])
    if cache is None:
        o = _attn_prompt(q_bf, k_bf.reshape(shp), v_bf, w["subln"], *lam, t=t_attn)
    else:
        o = _attn_sample(q_bf.reshape(shp), cache[0], cache[1], k_bf.reshape(shp), v_bf.reshape(shp),
                         w["subln"], *lam, tkb=min(2048, cache[0].shape[1])).reshape(n, D_MODEL)
    (y,) = _ffn(h1, w["g_ffn2"][1], bf["ffn_in11"], bf["ffn_out11"], tm=tm_ffn, pre=(o, bf["w_o"]),
                final_gain=w["g_final"])
    kv_shape = (b, s, N_HEADS, V_DIM)
    return y.reshape(shp), new_conv, k_new.reshape(kv_shape), v_new.reshape(kv_shape)


def kernel(x_prompt, x_sample, state_conv, cache_k, cache_v, norm_ffn1, norm_mix, norm_ffn2, ffn_w_in, ffn_w_out,
           conv_w_in, conv_w, conv_w_out, norm_kv, w_k, w_v, w_q, lambda_q1, lambda_k1, lambda_q2, lambda_k2,
           subln, w_o, norm_final):
    depth = norm_ffn1.shape[0]
    row = lambda a: a.reshape(1, -1)
    w = {
        "g_ffn1": [row(norm_ffn1[i]) for i in range(depth)],
        "g_mix": [row(norm_mix[i]) for i in range(depth)],
        "g_ffn2": [row(norm_ffn2[i]) for i in range(depth)],
        "conv_w": conv_w[0],
        "g_kv": row(norm_kv),
        "lq1": row(lambda_q1[0]), "lk1": row(lambda_k1[0]), "lq2": row(lambda_q2[0]), "lk2": row(lambda_k2[0]),
        "subln": row(subln[0]),
        "g_final": row(norm_final),
        "f32": {
            "conv_in": (conv_w_in, (0,)), "conv_out": (conv_w_out, (0,)), "w_q": (w_q, (0,)), "w_o": (w_o, (0,)),
            "ffn_in01": (ffn_w_in, (0, 1)), "ffn_out01": (ffn_w_out, (0, 1)),
            "ffn_in10": (ffn_w_in, (1, 0)), "ffn_out10": (ffn_w_out, (1, 0)),
            "ffn_in11": (ffn_w_in, (1, 1)), "ffn_out11": (ffn_w_out, (1, 1)),
        },
    }
    bf = {"ffn_in00": ffn_w_in[0, 0].astype(BF16), "ffn_out00": ffn_w_out[0, 0].astype(BF16),
          "w_kv": jnp.concatenate([w_k, w_v], axis=1).astype(BF16)}
    b = x_prompt.shape[0]
    conv0 = jnp.zeros((b, 1, CONV_W - 1, D_MODEL), x_prompt.dtype)
    y_p, conv_p, k_p, v_p = _trunk(x_prompt, conv0, None, w, bf, tm_ffn=512, tm_conv=512, t_attn=512)
    y_s, conv_s, k_s, v_s = _trunk(x_sample, state_conv, (cache_k, cache_v), w, bf, tm_ffn=512, tm_conv=512,
                                   t_attn=None)
    return (y_p, y_s, conv_p, k_p, v_p, conv_s, k_s, v_s)
```

```python
import functools
import math

import jax
import jax.numpy as jnp
from jax import lax
from jax.experimental import pallas as pl
from jax.experimental.pallas import tpu as pltpu

D_MODEL = 1024
CHUNK = 64
CONV_W = 3
D_FF = 2816
N_HEADS = 8
HEAD_DIM = 64
V_DIM = 2 * HEAD_DIM
EPS = 1e-6
NEG = -1e30
LAMBDA_INIT = 0.8 - 0.6 * math.exp(-0.3 * 1)
LOG2E = math.log2(math.e)
Q_SCALE = HEAD_DIM ** -0.5 * LOG2E

V7X_VMEM_BYTES = 64 * 1024 * 1024
VMEM_LIMIT = V7X_VMEM_BYTES - 8 * 1024 * 1024
LANES = 128

BF16 = jnp.bfloat16
F32 = jnp.float32


def _resident(shape):
    nd = len(shape)
    return pl.BlockSpec(shape, lambda *_: (0,) * nd, pipeline_mode=pl.Buffered(1))


BF16_SUBLANES = 16


def _cast_jobs(jobs, nsteps, step_of):
    args, in_specs, out_shapes, out_specs = [], [], [], []
    for array, lead in jobs:
        rows, cols = array.shape[len(lead):]
        nblk = nsteps
        while rows % nblk or (rows // nblk) % BF16_SUBLANES:
            assert nblk % 2 == 0, (array.shape, nsteps)
            nblk //= 2
        rep, br = nsteps // nblk, rows // nblk
        args.append(array)
        in_specs.append(pl.BlockSpec((None,) * len(lead) + (br, cols),
                                     lambda *ids, lead=tuple(lead), rep=rep: lead + (step_of(*ids) // rep, 0)))
        out_shapes.append(jax.ShapeDtypeStruct((rows, cols), BF16))
        out_specs.append(pl.BlockSpec((br, cols), lambda *ids, rep=rep: (step_of(*ids) // rep, 0)))
    return args, in_specs, out_shapes, out_specs


def _run_casts(src_refs, dst_refs):
    for src, dst in zip(src_refs, dst_refs):
        dst[...] = src[...].astype(BF16)


CAST_STEPS = 8


def _cast_call(jobs):
    args, in_specs, out_shapes, out_specs = _cast_jobs(jobs, CAST_STEPS, lambda i: i)
    return pl.pallas_call(
        lambda *refs: _run_casts(refs[:len(jobs)], refs[len(jobs):]),
        grid=(CAST_STEPS,),
        in_specs=in_specs,
        out_specs=out_specs,
        out_shape=out_shapes,
        compiler_params=pltpu.CompilerParams(dimension_semantics=("arbitrary",), vmem_limit_bytes=VMEM_LIMIT),
        name="cast",
    )(*args)


def _rms(x, g):
    return x * lax.rsqrt(jnp.mean(x * x, axis=-1, keepdims=True) + EPS) * g


def _dot(a, b):
    return jnp.dot(a, b, preferred_element_type=F32)


def _dot_nt(a, b):
    return lax.dot_general(a, b, (((1,), (1,)), ((), ())), preferred_element_type=F32)


def _ffn_kernel(*refs, pre_proj, final_norm, proj, n_cast):
    refs = list(refs)
    x_ref = refs.pop(0)
    if pre_proj:
        a_ref, wpre_ref = refs.pop(0), refs.pop(0)
    g_ref, win_ref, wout_ref = refs.pop(0), refs.pop(0), refs.pop(0)
    if final_norm:
        gf_ref = refs.pop(0)
    if proj is not None:
        gp_ref, wp_ref = refs.pop(0), refs.pop(0)
    cast_src = [refs.pop(0) for _ in range(n_cast)]
    y_ref = refs.pop(0)
    cast_dst = [refs.pop() for _ in range(n_cast)][::-1]
    _run_casts(cast_src, cast_dst)

    x = x_ref[...]
    if pre_proj == "rows":
        x = x + _dot(a_ref[...], wpre_ref[...])
    elif pre_proj == "cols":
        x = x + lax.dot_general(a_ref[0], wpre_ref[...], (((0,), (0,)), ((), ())), preferred_element_type=F32)
    xn = _rms(x, g_ref[...]).astype(BF16)
    gu = _dot(xn, win_ref[...])
    g, u = gu[:, :D_FF], gu[:, D_FF:]
    act = (g * (1.0 / (1.0 + jnp.exp(-g))) * u).astype(BF16)
    y = x + 0.5 * _dot(act, wout_ref[...])
    _ffn_outputs(y, y_ref, gf_ref if final_norm else None, proj, (gp_ref, wp_ref) if proj else None, refs)


def _ffn_outputs(y, y_ref, gf_ref, proj, proj_refs, out_refs):
    y_ref[...] = y if gf_ref is None else _rms(y, gf_ref[...])
    if proj is None:
        return
    gp_ref, wp_ref = proj_refs
    r = _dot(_rms(y, gp_ref[...]).astype(BF16), wp_ref[...])
    if proj in ("kv", "kvt"):
        k, v = r[:, :D_MODEL], r[:, D_MODEL:]
        kf_ref, vf_ref, kb_ref, vb_ref = out_refs
        if proj == "kv":
            vb_ref[...] = v.astype(BF16)
        else:
            vb_ref[0] = v.T.astype(BF16)
        kf_ref[...] = k
        vf_ref[...] = v
        kb_ref[...] = k.astype(BF16)
    else:
        (qb_ref,) = out_refs
        if proj == "q":
            qb_ref[...] = (r * Q_SCALE).astype(BF16)
        else:
            qb_ref[0] = (r * Q_SCALE).T.astype(BF16)


def _ffn(x, gain, w_in, w_out, *, tm, pre=None, final_gain=None, proj=None, proj_gain=None, proj_w=None,
         seq=None, casts=()):
    n = x.shape[0]
    tm = min(tm, n)
    row = lambda cols: pl.BlockSpec((tm, cols), lambda i: (i, 0))
    args, specs = [x], [row(D_MODEL)]
    pre_kind = None
    if pre is not None:
        a, w_pre = pre
        args += [a, w_pre]
        if a.ndim == 2:
            pre_kind = "rows"
            specs += [row(D_MODEL), _resident(w_pre.shape)]
        else:
            pre_kind = "cols"
            nsb_a = a.shape[2] // tm
            specs += [pl.BlockSpec((1, D_MODEL, tm), lambda i: (i // nsb_a, 0, i % nsb_a)), _resident(w_pre.shape)]
    args += [gain, w_in, w_out]
    specs += [_resident(gain.shape), _resident(w_in.shape), _resident(w_out.shape)]
    if final_gain is not None:
        args.append(final_gain)
        specs.append(_resident(final_gain.shape))
    out_shape = [jax.ShapeDtypeStruct((n, D_MODEL), F32)]
    out_specs = [row(D_MODEL)]
    if proj is not None:
        args += [proj_gain, proj_w]
        specs += [_resident(proj_gain.shape), _resident(proj_w.shape)]
        dts = {"kv": (F32, F32, BF16, BF16), "kvt": (F32, F32, BF16), "q": (BF16,), "qt": ()}[proj]
        out_shape += [jax.ShapeDtypeStruct((n, D_MODEL), dt) for dt in dts]
        out_specs += [row(D_MODEL) for _ in dts]
        if proj in ("kvt", "qt"):
            nsb = seq // tm
            out_shape.append(jax.ShapeDtypeStruct((n // seq, D_MODEL, seq), BF16))
            out_specs.append(pl.BlockSpec((1, D_MODEL, tm), lambda i: (i // nsb, 0, i % nsb)))
    c_args, c_in, c_shapes, c_out = _cast_jobs(casts, n // tm, lambda i: i)
    args += c_args
    specs += c_in
    out_shape += c_shapes
    out_specs += c_out
    return pl.pallas_call(
        functools.partial(_ffn_kernel, pre_proj=pre_kind, final_norm=final_gain is not None, proj=proj,
                          n_cast=len(casts)),
        grid=(n // tm,),
        in_specs=specs,
        out_specs=out_specs,
        out_shape=out_shape,
        compiler_params=pltpu.CompilerParams(dimension_semantics=("arbitrary",), vmem_limit_bytes=VMEM_LIMIT),
        name="ffn" + ("_pre" if pre is not None else "") + ("_" + proj if proj else "") + ("_fin" if final_gain is not None else ""),
    )(*args)


def _conv_kernel(x_ref, st_ref, g_ref, win_ref, cw_ref, wout_ref, *refs, n_cast):
    cast_src, (o_ref, ns_ref), cast_dst, (carry,) = (refs[:n_cast], refs[n_cast:n_cast + 2],
                                                     refs[n_cast + 2:2 * n_cast + 2], refs[2 * n_cast + 2:])
    _run_casts(cast_src, cast_dst)
    tm = x_ref.shape[1]

    @pl.when(pl.program_id(1) == 0)
    def _():
        carry[0:CONV_W - 1, :] = st_ref[0, 0]

    h = x_ref[0]
    hn = _rms(h, g_ref[...]).astype(BF16)
    proj = _dot(hn, win_ref[...])
    b_g, c_g, hh = proj[:, :D_MODEL], proj[:, D_MODEL:2 * D_MODEL], proj[:, 2 * D_MODEL:]
    u = c_g * hh
    prev0, prev1 = carry[0:1, :], carry[1:2, :]
    row = lax.broadcasted_iota(jnp.int32, (tm, D_MODEL), 0)
    r1 = pltpu.roll(u, 1, 0)
    r2 = pltpu.roll(u, 2, 0)
    u1 = jnp.where(row == 0, prev1, r1)
    u2 = jnp.where(row == 0, prev0, jnp.where(row == 1, prev1, r2))
    cw = cw_ref[...]
    conv = cw[0:1, :] * u2 + cw[1:2, :] * u1 + cw[2:3, :] * u
    o_ref[0] = h + _dot((b_g * conv).astype(BF16), wout_ref[...])
    tail = r2[0:CONV_W - 1, :]
    carry[0:CONV_W - 1, :] = tail
    ns_ref[0, 0] = tail


def _conv_streams_kernel(x_ref, st_ref, g_ref, win_ref, cw_ref, wout_ref, o_ref, ns_ref, *, s):
    n = x_ref.shape[0]
    b = n // s
    h = x_ref[...]
    hn = _rms(h, g_ref[...]).astype(BF16)
    proj = _dot(hn, win_ref[...])
    b_g, c_g, hh = proj[:, :D_MODEL], proj[:, D_MODEL:2 * D_MODEL], proj[:, 2 * D_MODEL:]
    u = c_g * hh
    st = st_ref[...]
    per_row = lambda rows: jnp.broadcast_to(rows, (b, s, D_MODEL)).reshape(n, D_MODEL)
    prev0, prev1 = per_row(st[:, 0:1, :]), per_row(st[:, 1:2, :])
    pos = lax.broadcasted_iota(jnp.int32, (n, D_MODEL), 0) & (s - 1)
    u1 = jnp.where(pos == 0, prev1, pltpu.roll(u, 1, 0))
    u2 = jnp.where(pos == 0, prev0, jnp.where(pos == 1, prev1, pltpu.roll(u, 2, 0)))
    cw = cw_ref[...]
    conv = cw[0:1, :] * u2 + cw[1:2, :] * u1 + cw[2:3, :] * u
    o_ref[...] = h + _dot((b_g * conv).astype(BF16), wout_ref[...])
    ns_ref[...] = pltpu.roll(u, n - (s - (CONV_W - 1)), 0).reshape(b, s, D_MODEL)[:, 0:CONV_W - 1, :]


def _conv_streams(x, state, gain, w_in, conv_w, w_out):
    b, s, _ = x.shape
    assert s & (s - 1) == 0 and s % 8 == 0, s
    n = b * s
    full = lambda shape: pl.BlockSpec(shape, lambda i: (0,) * len(shape))
    y, ns = pl.pallas_call(
        functools.partial(_conv_streams_kernel, s=s),
        grid=(1,),
        in_specs=[full((n, D_MODEL)), full((b, CONV_W - 1, D_MODEL)),
                  full(gain.shape), full(w_in.shape), full(conv_w.shape), full(w_out.shape)],
        out_specs=[full((n, D_MODEL)), full((b, CONV_W - 1, D_MODEL))],
        out_shape=[jax.ShapeDtypeStruct((n, D_MODEL), F32), jax.ShapeDtypeStruct((b, CONV_W - 1, D_MODEL), F32)],
        compiler_params=pltpu.CompilerParams(dimension_semantics=("arbitrary",), vmem_limit_bytes=VMEM_LIMIT),
        name="conv_streams",
    )(x.reshape(n, D_MODEL), state.reshape(b, CONV_W - 1, D_MODEL), gain, w_in, conv_w, w_out)
    return y.reshape(b, s, D_MODEL), ns.reshape(b, 1, CONV_W - 1, D_MODEL)


def _conv(x, state, gain, w_in, conv_w, w_out, *, tm, casts=()):
    b, s, _ = x.shape
    tm = min(tm, s)
    nsb = s // tm
    c_args, c_in, c_shapes, c_out = _cast_jobs(casts, b * nsb, lambda bi, i: bi * nsb + i)
    return pl.pallas_call(
        functools.partial(_conv_kernel, n_cast=len(casts)),
        grid=(b, nsb),
        in_specs=[
            pl.BlockSpec((1, tm, D_MODEL), lambda bi, i: (bi, i, 0)),
            pl.BlockSpec((1, 1, CONV_W - 1, D_MODEL), lambda bi, i: (bi, 0, 0, 0)),
            _resident(gain.shape), _resident(w_in.shape), _resident(conv_w.shape), _resident(w_out.shape),
        ] + c_in,
        out_specs=[
            pl.BlockSpec((1, tm, D_MODEL), lambda bi, i: (bi, i, 0)),
            pl.BlockSpec((1, 1, CONV_W - 1, D_MODEL), lambda bi, i: (bi, 0, 0, 0)),
        ] + c_out,
        out_shape=[jax.ShapeDtypeStruct((b, s, D_MODEL), F32),
                   jax.ShapeDtypeStruct((b, 1, CONV_W - 1, D_MODEL), F32)] + c_shapes,
        scratch_shapes=[pltpu.VMEM((8, D_MODEL), F32)],
        compiler_params=pltpu.CompilerParams(dimension_semantics=("arbitrary", "arbitrary"),
                                             vmem_limit_bytes=VMEM_LIMIT),
        name="conv",
    )(x, state, gain, w_in, conv_w, w_out, *c_args)


def _stack_q(q):
    lane = lax.broadcasted_iota(jnp.int32, q.shape, 1)
    zero = jnp.zeros_like(q)
    return jnp.concatenate([jnp.where(lane < HEAD_DIM, q, zero), jnp.where(lane >= HEAD_DIM, q, zero)], axis=0)


def _chunk_of(pos):
    return lax.shift_right_logical(pos, CHUNK.bit_length() - 1)


def _lambda(lq1, lk1, lq2, lk2):
    return (jnp.exp(jnp.sum(lq1 * lk1, axis=-1, keepdims=True))
            - jnp.exp(jnp.sum(lq2 * lk2, axis=-1, keepdims=True)) + LAMBDA_INIT)


def _finish(acc, l, t, lam, subln):
    o = acc[:t] / l[:t] - lam * (acc[t:] / l[t:])
    return (_rms(o, subln) * (1.0 - LAMBDA_INIT)).astype(BF16)


def _softmax_init(m_ref, l_ref, acc_ref):
    m_ref[...] = jnp.full(m_ref.shape, NEG, F32)
    l_ref[...] = jnp.zeros(l_ref.shape, F32)
    acc_ref[...] = jnp.zeros(acc_ref.shape, F32)


def _softmax_update(s, v, m_ref, l_ref, acc_ref):
    cols = [s[:, c * LANES:(c + 1) * LANES] for c in range(s.shape[1] // LANES)]
    m_prev = m_ref[...]
    m_next = jnp.maximum(m_prev, jnp.max(functools.reduce(jnp.maximum, cols), axis=1, keepdims=True))
    alpha = jnp.exp2(m_prev - m_next)
    ps = [jnp.exp2(c - m_next) for c in cols]
    l_ref[...] = alpha * l_ref[...] + functools.reduce(jnp.add, ps)
    p = jnp.concatenate([c.astype(BF16) for c in ps], axis=1)
    acc_ref[...] = alpha * acc_ref[...] + _dot(p, v)
    m_ref[...] = m_next


ONES_ROWS = 16


def _softmax_update_t(s, cmax, vt, m_ref, acc_ref):
    m_prev = m_ref[...]
    m_next = jnp.maximum(m_prev, cmax)
    alpha = jnp.exp2(m_prev - m_next)
    p = jnp.exp2(s - m_next).astype(BF16)
    vt1 = jnp.concatenate([vt, jnp.ones((ONES_ROWS, vt.shape[1]), BF16)], axis=0)
    acc_ref[...] = alpha * acc_ref[...] + _dot(vt1, p)
    m_ref[...] = m_next


SLOT_A, SLOT_B = 0, 1
FULL_UNROLLS = (8, 4)
DIAG_UNROLLS = (4, 2)


def _attn_prompt_kernel(fi_ref, fj_ref, qt_ref, k_ref, vt_ref, sub_ref, lq1_ref, lk1_ref, lq2_ref, lk2_ref, o_ref,
                        qs_sc, s_sc, cm_sc, m_sc, acc_sc, *, t):
    nq = qt_ref.shape[2] // t
    nfull = nq * (nq - 1) // 2
    feat = lax.broadcasted_iota(jnp.int32, (V_DIM, t), 0)
    for i in range(nq):
        qt = qt_ref[0, :, i * t:(i + 1) * t]
        zero = jnp.zeros_like(qt)
        qs_sc[i] = jnp.concatenate([jnp.where(feat < HEAD_DIM, qt, zero), jnp.where(feat >= HEAD_DIM, qt, zero)],
                                   axis=1)
    m_sc[...] = jnp.full(m_sc.shape, NEG, F32)
    acc_sc[...] = jnp.zeros(acc_sc.shape, F32)

    def S(i, j, slot, masked):
        kj = k_ref[0, pl.ds(pl.multiple_of(j * t, t), t), :]
        s = _dot(kj, qs_sc[i])
        if masked:
            key = lax.broadcasted_iota(jnp.int32, s.shape, 0)
            qry = lax.broadcasted_iota(jnp.int32, s.shape, 1)
            s = jnp.where(_chunk_of(key) <= _chunk_of(qry & (t - 1)), s, NEG)
        s_sc[slot] = s
        cm_sc[slot] = jnp.max(s, axis=0, keepdims=True)

    def C(i, j, slot):
        vtj = vt_ref[0, :, pl.ds(pl.multiple_of(j * t, t), t)]
        _softmax_update_t(s_sc[slot], cm_sc[slot], vtj, m_sc.at[i], acc_sc.at[i])

    s_full = lambda n, slot: S(fi_ref[n], fj_ref[n], slot, False)
    c_full = lambda n, slot: C(fi_ref[n], fj_ref[n], slot)
    s_diag = lambda d, slot: S(d, d, slot, True)
    c_diag = lambda d, slot: C(d, d, slot)

    if nfull:
        s_full(0, SLOT_A)
    else:
        s_diag(0, SLOT_A)
    slots = (SLOT_A, SLOT_B)
    done = 0
    for unroll in FULL_UNROLLS:
        trips = max(nfull - 1 - done, 0) // unroll

        def body(it, c, unroll=unroll, base=done):
            n = base + unroll * it
            for r in range(unroll):
                s_full(n + r + 1, slots[(r + 1) % 2])
                c_full(n + r, slots[r % 2])
            return c

        lax.fori_loop(0, trips, body, 0)
        done += unroll * trips
    for r, n in enumerate(range(done, nfull)):
        cur, nxt = slots[r % 2], slots[(r + 1) % 2]
        if n + 1 < nfull:
            s_full(n + 1, nxt)
        else:
            s_diag(0, nxt)
        c_full(n, cur)
    first, second = slots[(nfull - done) % 2], slots[(nfull - done + 1) % 2]

    slots = (first, second)
    done = 0
    for unroll in DIAG_UNROLLS:
        trips = max(nq - 1 - done, 0) // unroll

        def body(it, c, unroll=unroll, base=done):
            d = base + unroll * it
            for r in range(unroll):
                s_diag(d + r + 1, slots[(r + 1) % 2])
                c_diag(d + r, slots[r % 2])
            return c

        lax.fori_loop(0, trips, body, 0)
        done += unroll * trips
    for r, d in enumerate(range(done, nq)):
        if d + 1 < nq:
            s_diag(d + 1, slots[(r + 1) % 2])
        c_diag(d, slots[r % 2])

    lam = _lambda(lq1_ref[...], lk1_ref[...], lq2_ref[...], lk2_ref[...])
    gain = sub_ref[...] * (1.0 - LAMBDA_INIT)

    def finish(i, c):
        acc = acc_sc[i]
        on = acc[0:V_DIM, :] * (1.0 / acc[V_DIM:V_DIM + 1, :])
        o = on[:, :t] - lam * on[:, t:]
        inv = lax.rsqrt(jnp.mean(o * o, axis=0, keepdims=True) + EPS)
        o_ref[0, :, pl.ds(pl.multiple_of(i * t, t), t)] = (o * inv * gain).astype(BF16)
        return c

    lax.fori_loop(0, nq, finish, 0)


def _attn_prompt(qt, k, vt, subln, lq1, lk1, lq2, lk2, *, t):
    b, s, _ = k.shape
    nq = s // t
    pairs = [(i, j) for i in range(1, nq) for j in range(i)] or [(0, 0)]
    fi = jnp.asarray([p[0] for p in pairs], jnp.int32)
    fj = jnp.asarray([p[1] for p in pairs], jnp.int32)
    small = lambda a: pl.BlockSpec(a.shape, lambda bi, h, *_: (0,) * a.ndim, pipeline_mode=pl.Buffered(1))
    subln = subln.reshape(V_DIM, 1)
    rows = pl.BlockSpec((1, s, V_DIM), lambda bi, h, *_: (bi, 0, h))
    cols = pl.BlockSpec((1, V_DIM, s), lambda bi, h, *_: (bi, h, 0))
    return pl.pallas_call(
        functools.partial(_attn_prompt_kernel, t=t),
        grid_spec=pltpu.PrefetchScalarGridSpec(
            num_scalar_prefetch=2,
            grid=(b, N_HEADS),
            in_specs=[cols, rows, cols, small(subln), small(lq1), small(lk1), small(lq2), small(lk2)],
            out_specs=cols,
            scratch_shapes=[pltpu.VMEM((nq, V_DIM, 2 * t), BF16), pltpu.VMEM((2, t, 2 * t), F32),
                            pltpu.VMEM((2, 1, 2 * t), F32), pltpu.VMEM((nq, 1, 2 * t), F32),
                            pltpu.VMEM((nq, V_DIM + ONES_ROWS, 2 * t), F32)],
        ),
        out_shape=jax.ShapeDtypeStruct((b, D_MODEL, s), BF16),
        compiler_params=pltpu.CompilerParams(dimension_semantics=("arbitrary",) * 2, vmem_limit_bytes=VMEM_LIMIT),
        name="attn_prompt",
    )(fi, fj, qt, k, vt, subln, lq1, lk1, lq2, lk2)


def _attn_sample_kernel(q_ref, kc_ref, vc_ref, kn_ref, vn_ref, sub_ref, lq1_ref, lk1_ref, lq2_ref, lk2_ref,
                        o_ref, m_sc, l_sc, acc_sc, *, q_start):
    t = q_ref.shape[1]
    tkb = kc_ref.shape[1] // N_HEADS
    j = pl.program_id(1)

    @pl.when(j == 0)
    def _():
        _softmax_init(m_sc, l_sc, acc_sc)

    head_cols = lambda h: slice(h * V_DIM, (h + 1) * V_DIM)
    scores = []
    for h in range(N_HEADS):
        qs = _stack_q(q_ref[0, :, head_cols(h)])
        kh = kc_ref[0, pl.ds(h, tkb, stride=N_HEADS), :].astype(BF16)
        scores.append(_dot_nt(qs, kh))
    for h in range(N_HEADS):
        vh = vc_ref[0, pl.ds(h, tkb, stride=N_HEADS), :].astype(BF16)
        _softmax_update(scores[h], vh, m_sc.at[h], l_sc.at[h], acc_sc.at[h])

    @pl.when(j == pl.num_programs(1) - 1)
    def _():
        lam = _lambda(lq1_ref[...], lk1_ref[...], lq2_ref[...], lk2_ref[...])
        for h in range(N_HEADS):
            qs = _stack_q(q_ref[0, :, head_cols(h)])
            s_n = _dot_nt(qs, kn_ref[0, :, head_cols(h)])
            row = lax.broadcasted_iota(jnp.int32, s_n.shape, 0)
            col = lax.broadcasted_iota(jnp.int32, s_n.shape, 1)
            s_n = jnp.where(_chunk_of(q_start + col) <= _chunk_of(q_start + (row & (t - 1))), s_n, NEG)
            m_prev = m_sc[h]
            m_fin = jnp.maximum(m_prev, jnp.max(s_n, axis=1, keepdims=True))
            alpha = jnp.exp2(m_prev - m_fin)
            p_n = jnp.exp2(s_n - m_fin[:, :t])
            l = jnp.sum(alpha * l_sc[h], axis=1, keepdims=True) + jnp.sum(p_n, axis=1, keepdims=True)
            acc = alpha * acc_sc[h] + _dot(p_n.astype(BF16), vn_ref[0, :, head_cols(h)])
            o_ref[0, :, head_cols(h)] = _finish(acc, l, t, lam, sub_ref[...])


def _attn_sample(q, cache_k, cache_v, kn, vn, subln, lq1, lk1, lq2, lk2, *, tkb):
    b, t, _ = q.shape
    past = cache_k.shape[1]
    kc = cache_k.reshape(b, past * N_HEADS, V_DIM)
    vc = cache_v.reshape(b, past * N_HEADS, V_DIM)
    small = lambda a: _resident(a.shape)
    rows = pl.BlockSpec((1, t, D_MODEL), lambda bi, j: (bi, 0, 0))
    blk = pl.BlockSpec((1, tkb * N_HEADS, V_DIM), lambda bi, j: (bi, j, 0))
    return pl.pallas_call(
        functools.partial(_attn_sample_kernel, q_start=past),
        grid=(b, past // tkb),
        in_specs=[rows, blk, blk, rows, rows, small(subln), small(lq1), small(lk1), small(lq2), small(lk2)],
        out_specs=rows,
        out_shape=jax.ShapeDtypeStruct((b, t, D_MODEL), BF16),
        scratch_shapes=[pltpu.VMEM((N_HEADS, 2 * t, LANES), F32), pltpu.VMEM((N_HEADS, 2 * t, LANES), F32),
                        pltpu.VMEM((N_HEADS, 2 * t, V_DIM), F32)],
        compiler_params=pltpu.CompilerParams(dimension_semantics=("arbitrary",) * 2, vmem_limit_bytes=VMEM_LIMIT),
        name="attn_sample",
    )(q, kc, vc, kn, vn, subln, lq1, lk1, lq2, lk2)


def _trunk(x, conv_state, cache, w, bf, *, tm_ffn, tm_conv, t_attn):
    b, s, _ = x.shape
    n = b * s

    def with_casts(call, n_out, *names):
        todo = [name for name in names if name not in bf]
        outs = call([w["f32"][name] for name in todo])
        bf.update(zip(todo, outs[n_out:]))
        return outs[:n_out]

    (h,) = with_casts(lambda casts: _ffn(x.reshape(n, D_MODEL), w["g_ffn1"][0], bf["ffn_in00"], bf["ffn_out00"],
                                         tm=tm_ffn, casts=casts),
                      1, "conv_in", "conv_out", "ffn_in01", "ffn_out01")
    conv_args = (h.reshape(b, s, D_MODEL), conv_state, w["g_mix"][0], bf["conv_in"], w["conv_w"], bf["conv_out"])
    if s < tm_conv:
        h, new_conv = _conv_streams(*conv_args)
    else:
        h, new_conv = with_casts(lambda casts: _conv(*conv_args, tm=tm_conv, casts=casts),
                                 2, "w_q", "ffn_in10", "ffn_out10")
    x1, k_new, v_new, k_bf, v_bf = with_casts(
        lambda casts: _ffn(h.reshape(n, D_MODEL), w["g_ffn2"][0], bf["ffn_in01"], bf["ffn_out01"], tm=tm_ffn,
                           proj="kvt" if cache is None else "kv", proj_gain=w["g_kv"], proj_w=bf["w_kv"], seq=s,
                           casts=casts),
        5, "ffn_in11", "ffn_out11")
    h1, q_bf = with_casts(
        lambda casts: _ffn(x1, w["g_ffn1"][1], bf["ffn_in10"], bf["ffn_out10"], tm=tm_ffn,
                           proj="qt" if cache is None else "q", proj_gain=w["g_mix"][1], proj_w=bf["w_q"], seq=s,
                           casts=casts),
        2, "w_o")
    shp = (b, s, D_MODEL)
    lam = (w["lq1"], w["lk1"], w["lq2"], w["lk2"])
    if cache is None:
        o = _attn_prompt(q_bf, k_bf.reshape(shp), v_bf, w["subln"], *lam, t=t_attn)
    else:
        o = _attn_sample(q_bf.reshape(shp), cache[0], cache[1], k_bf.reshape(shp), v_bf.reshape(shp),
                         w["subln"], *lam, tkb=min(2048, cache[0].shape[1])).reshape(n, D_MODEL)
    (y,) = _ffn(h1, w["g_ffn2"][1], bf["ffn_in11"], bf["ffn_out11"], tm=tm_ffn, pre=(o, bf["w_o"]),
                final_gain=w["g_final"])
    kv_shape = (b, s, N_HEADS, V_DIM)
    return y.reshape(shp), new_conv, k_new.reshape(kv_shape), v_new.reshape(kv_shape)


def kernel(x_prompt, x_sample, state_conv, cache_k, cache_v, norm_ffn1, norm_mix, norm_ffn2, ffn_w_in, ffn_w_out,
           conv_w_in, conv_w, conv_w_out, norm_kv, w_k, w_v, w_q, lambda_q1, lambda_k1, lambda_q2, lambda_k2,
           subln, w_o, norm_final):
    depth = norm_ffn1.shape[0]
    row = lambda a: a.reshape(1, -1)
    w = {
        "g_ffn1": [row(norm_ffn1[i]) for i in range(depth)],
        "g_mix": [row(norm_mix[i]) for i in range(depth)],
        "g_ffn2": [row(norm_ffn2[i]) for i in range(depth)],
        "conv_w": conv_w[0],
        "g_kv": row(norm_kv),
        "lq1": row(lambda_q1[0]), "lk1": row(lambda_k1[0]), "lq2": row(lambda_q2[0]), "lk2": row(lambda_k2[0]),
        "subln": row(subln[0]),
        "g_final": row(norm_final),
        "f32": {
            "conv_in": (conv_w_in, (0,)), "conv_out": (conv_w_out, (0,)), "w_q": (w_q, (0,)), "w_o": (w_o, (0,)),
            "ffn_in01": (ffn_w_in, (0, 1)), "ffn_out01": (ffn_w_out, (0, 1)),
            "ffn_in10": (ffn_w_in, (1, 0)), "ffn_out10": (ffn_w_out, (1, 0)),
            "ffn_in11": (ffn_w_in, (1, 1)), "ffn_out11": (ffn_w_out, (1, 1)),
        },
    }
    ffn_in00, ffn_out00 = _cast_call([(ffn_w_in, (0, 0)), (ffn_w_out, (0, 0))])
    bf = {"ffn_in00": ffn_in00, "ffn_out00": ffn_out00,
          "w_kv": jnp.concatenate([w_k, w_v], axis=1).astype(BF16)}
    b = x_prompt.shape[0]
    conv0 = jnp.zeros((b, 1, CONV_W - 1, D_MODEL), x_prompt.dtype)
    y_p, conv_p, k_p, v_p = _trunk(x_prompt, conv0, None, w, bf, tm_ffn=512, tm_conv=512, t_attn=512)
    y_s, conv_s, k_s, v_s = _trunk(x_sample, state_conv, (cache_k, cache_v), w, bf, tm_ffn=512, tm_conv=512,
                                   t_attn=None)
    return (y_p, y_s, conv_p, k_p, v_p, conv_s, k_s, v_s)
```

```python
import functools
import math

import jax
import jax.numpy as jnp
from jax import lax
from jax.experimental import pallas as pl
from jax.experimental.pallas import tpu as pltpu

D_MODEL = 1024
CHUNK = 64
CONV_W = 3
D_FF = 2816
N_HEADS = 8
HEAD_DIM = 64
V_DIM = 2 * HEAD_DIM
EPS = 1e-6
NEG = -1e30
LAMBDA_INIT = 0.8 - 0.6 * math.exp(-0.3 * 1)
LOG2E = math.log2(math.e)
Q_SCALE = HEAD_DIM ** -0.5 * LOG2E

V7X_VMEM_BYTES = 64 * 1024 * 1024
VMEM_LIMIT = V7X_VMEM_BYTES - 8 * 1024 * 1024
LANES = 128

BF16 = jnp.bfloat16
F32 = jnp.float32


def _resident(shape):
    nd = len(shape)
    return pl.BlockSpec(shape, lambda *_: (0,) * nd, pipeline_mode=pl.Buffered(1))


BF16_SUBLANES = 16


def _cast_jobs(jobs, nsteps, step_of):
    args, in_specs, out_shapes, out_specs = [], [], [], []
    for array, lead in jobs:
        rows, cols = array.shape[len(lead):]
        nblk = nsteps
        while rows % nblk or (rows // nblk) % BF16_SUBLANES:
            assert nblk % 2 == 0, (array.shape, nsteps)
            nblk //= 2
        rep, br = nsteps // nblk, rows // nblk
        args.append(array)
        in_specs.append(pl.BlockSpec((None,) * len(lead) + (br, cols),
                                     lambda *ids, lead=tuple(lead), rep=rep: lead + (step_of(*ids) // rep, 0)))
        out_shapes.append(jax.ShapeDtypeStruct((rows, cols), BF16))
        out_specs.append(pl.BlockSpec((br, cols), lambda *ids, rep=rep: (step_of(*ids) // rep, 0)))
    return args, in_specs, out_shapes, out_specs


def _run_casts(src_refs, dst_refs):
    for src, dst in zip(src_refs, dst_refs):
        dst[...] = src[...].astype(BF16)


CAST_STEPS = 8


def _cast_call(jobs):
    args, in_specs, out_shapes, out_specs = _cast_jobs(jobs, CAST_STEPS, lambda i: i)
    return pl.pallas_call(
        lambda *refs: _run_casts(refs[:len(jobs)], refs[len(jobs):]),
        grid=(CAST_STEPS,),
        in_specs=in_specs,
        out_specs=out_specs,
        out_shape=out_shapes,
        compiler_params=pltpu.CompilerParams(dimension_semantics=("arbitrary",), vmem_limit_bytes=VMEM_LIMIT),
        name="cast",
    )(*args)


def _rms(x, g):
    return x * lax.rsqrt(jnp.mean(x * x, axis=-1, keepdims=True) + EPS) * g


def _dot(a, b):
    return jnp.dot(a, b, preferred_element_type=F32)


def _dot_nt(a, b):
    return lax.dot_general(a, b, (((1,), (1,)), ((), ())), preferred_element_type=F32)


def _ffn_kernel(*refs, pre_proj, final_norm, proj, n_cast):
    refs = list(refs)
    x_ref = refs.pop(0)
    if pre_proj:
        a_ref, wpre_ref = refs.pop(0), refs.pop(0)
    g_ref, win_ref, wout_ref = refs.pop(0), refs.pop(0), refs.pop(0)
    if final_norm:
        gf_ref = refs.pop(0)
    if proj is not None:
        gp_ref, wp_ref = refs.pop(0), refs.pop(0)
    cast_src = [refs.pop(0) for _ in range(n_cast)]
    y_ref = refs.pop(0)
    cast_dst = [refs.pop() for _ in range(n_cast)][::-1]
    _run_casts(cast_src, cast_dst)

    x = x_ref[...]
    if pre_proj == "rows":
        x = x + _dot(a_ref[...], wpre_ref[...])
    elif pre_proj == "cols":
        x = x + lax.dot_general(a_ref[0], wpre_ref[...], (((0,), (0,)), ((), ())), preferred_element_type=F32)
    xn = _rms(x, g_ref[...]).astype(BF16)
    gu = _dot(xn, win_ref[...])
    g, u = gu[:, :D_FF], gu[:, D_FF:]
    act = (g * (1.0 / (1.0 + jnp.exp(-g))) * u).astype(BF16)
    y = x + 0.5 * _dot(act, wout_ref[...])
    _ffn_outputs(y, y_ref, gf_ref if final_norm else None, proj, (gp_ref, wp_ref) if proj else None, refs)


def _ffn_outputs(y, y_ref, gf_ref, proj, proj_refs, out_refs):
    y_ref[...] = y if gf_ref is None else _rms(y, gf_ref[...])
    if proj is None:
        return
    gp_ref, wp_ref = proj_refs
    r = _dot(_rms(y, gp_ref[...]).astype(BF16), wp_ref[...])
    if proj in ("kv", "kvt"):
        k, v = r[:, :D_MODEL], r[:, D_MODEL:]
        kf_ref, vf_ref, kb_ref, vb_ref = out_refs
        if proj == "kv":
            vb_ref[...] = v.astype(BF16)
        else:
            vb_ref[0] = v.T.astype(BF16)
        kf_ref[...] = k
        vf_ref[...] = v
        kb_ref[...] = k.astype(BF16)
    else:
        (qb_ref,) = out_refs
        if proj == "q":
            qb_ref[...] = (r * Q_SCALE).astype(BF16)
        else:
            qb_ref[0] = (r * Q_SCALE).T.astype(BF16)


def _ffn(x, gain, w_in, w_out, *, tm, pre=None, final_gain=None, proj=None, proj_gain=None, proj_w=None,
         seq=None, casts=()):
    n = x.shape[0]
    tm = min(tm, n)
    row = lambda cols: pl.BlockSpec((tm, cols), lambda i: (i, 0))
    args, specs = [x], [row(D_MODEL)]
    pre_kind = None
    if pre is not None:
        a, w_pre = pre
        args += [a, w_pre]
        if a.ndim == 2:
            pre_kind = "rows"
            specs += [row(D_MODEL), _resident(w_pre.shape)]
        else:
            pre_kind = "cols"
            nsb_a = a.shape[2] // tm
            specs += [pl.BlockSpec((1, D_MODEL, tm), lambda i: (i // nsb_a, 0, i % nsb_a)), _resident(w_pre.shape)]
    args += [gain, w_in, w_out]
    specs += [_resident(gain.shape), _resident(w_in.shape), _resident(w_out.shape)]
    if final_gain is not None:
        args.append(final_gain)
        specs.append(_resident(final_gain.shape))
    out_shape = [jax.ShapeDtypeStruct((n, D_MODEL), F32)]
    out_specs = [row(D_MODEL)]
    if proj is not None:
        args += [proj_gain, proj_w]
        specs += [_resident(proj_gain.shape), _resident(proj_w.shape)]
        dts = {"kv": (F32, F32, BF16, BF16), "kvt": (F32, F32, BF16), "q": (BF16,), "qt": ()}[proj]
        out_shape += [jax.ShapeDtypeStruct((n, D_MODEL), dt) for dt in dts]
        out_specs += [row(D_MODEL) for _ in dts]
        if proj in ("kvt", "qt"):
            nsb = seq // tm
            out_shape.append(jax.ShapeDtypeStruct((n // seq, D_MODEL, seq), BF16))
            out_specs.append(pl.BlockSpec((1, D_MODEL, tm), lambda i: (i // nsb, 0, i % nsb)))
    c_args, c_in, c_shapes, c_out = _cast_jobs(casts, n // tm, lambda i: i)
    args += c_args
    specs += c_in
    out_shape += c_shapes
    out_specs += c_out
    return pl.pallas_call(
        functools.partial(_ffn_kernel, pre_proj=pre_kind, final_norm=final_gain is not None, proj=proj,
                          n_cast=len(casts)),
        grid=(n // tm,),
        in_specs=specs,
        out_specs=out_specs,
        out_shape=out_shape,
        compiler_params=pltpu.CompilerParams(dimension_semantics=("arbitrary",), vmem_limit_bytes=VMEM_LIMIT),
        name="ffn" + ("_pre" if pre is not None else "") + ("_" + proj if proj else "") + ("_fin" if final_gain is not None else ""),
    )(*args)


def _conv_kernel(x_ref, st_ref, g_ref, win_ref, cw_ref, wout_ref, *refs, n_cast):
    cast_src, (o_ref, ns_ref), cast_dst, (carry,) = (refs[:n_cast], refs[n_cast:n_cast + 2],
                                                     refs[n_cast + 2:2 * n_cast + 2], refs[2 * n_cast + 2:])
    _run_casts(cast_src, cast_dst)
    tm = x_ref.shape[1]

    @pl.when(pl.program_id(1) == 0)
    def _():
        carry[0:CONV_W - 1, :] = st_ref[0, 0]

    h = x_ref[0]
    hn = _rms(h, g_ref[...]).astype(BF16)
    proj = _dot(hn, win_ref[...])
    b_g, c_g, hh = proj[:, :D_MODEL], proj[:, D_MODEL:2 * D_MODEL], proj[:, 2 * D_MODEL:]
    u = c_g * hh
    prev0, prev1 = carry[0:1, :], carry[1:2, :]
    row = lax.broadcasted_iota(jnp.int32, (tm, D_MODEL), 0)
    r1 = pltpu.roll(u, 1, 0)
    r2 = pltpu.roll(u, 2, 0)
    u1 = jnp.where(row == 0, prev1, r1)
    u2 = jnp.where(row == 0, prev0, jnp.where(row == 1, prev1, r2))
    cw = cw_ref[...]
    conv = cw[0:1, :] * u2 + cw[1:2, :] * u1 + cw[2:3, :] * u
    o_ref[0] = h + _dot((b_g * conv).astype(BF16), wout_ref[...])
    tail = r2[0:CONV_W - 1, :]
    carry[0:CONV_W - 1, :] = tail
    ns_ref[0, 0] = tail


def _conv_streams_kernel(x_ref, st_ref, g_ref, win_ref, cw_ref, wout_ref, o_ref, ns_ref, *, s):
    n = x_ref.shape[0]
    b = n // s
    h = x_ref[...]
    hn = _rms(h, g_ref[...]).astype(BF16)
    proj = _dot(hn, win_ref[...])
    b_g, c_g, hh = proj[:, :D_MODEL], proj[:, D_MODEL:2 * D_MODEL], proj[:, 2 * D_MODEL:]
    u = c_g * hh
    st = st_ref[...]
    per_row = lambda rows: jnp.broadcast_to(rows, (b, s, D_MODEL)).reshape(n, D_MODEL)
    prev0, prev1 = per_row(st[:, 0:1, :]), per_row(st[:, 1:2, :])
    pos = lax.broadcasted_iota(jnp.int32, (n, D_MODEL), 0) & (s - 1)
    u1 = jnp.where(pos == 0, prev1, pltpu.roll(u, 1, 0))
    u2 = jnp.where(pos == 0, prev0, jnp.where(pos == 1, prev1, pltpu.roll(u, 2, 0)))
    cw = cw_ref[...]
    conv = cw[0:1, :] * u2 + cw[1:2, :] * u1 + cw[2:3, :] * u
    o_ref[...] = h + _dot((b_g * conv).astype(BF16), wout_ref[...])
    ns_ref[...] = pltpu.roll(u, n - (s - (CONV_W - 1)), 0).reshape(b, s, D_MODEL)[:, 0:CONV_W - 1, :]


def _conv_streams(x, state, gain, w_in, conv_w, w_out):
    b, s, _ = x.shape
    assert s & (s - 1) == 0 and s % 8 == 0, s
    n = b * s
    full = lambda shape: pl.BlockSpec(shape, lambda i: (0,) * len(shape))
    y, ns = pl.pallas_call(
        functools.partial(_conv_streams_kernel, s=s),
        grid=(1,),
        in_specs=[full((n, D_MODEL)), full((b, CONV_W - 1, D_MODEL)),
                  full(gain.shape), full(w_in.shape), full(conv_w.shape), full(w_out.shape)],
        out_specs=[full((n, D_MODEL)), full((b, CONV_W - 1, D_MODEL))],
        out_shape=[jax.ShapeDtypeStruct((n, D_MODEL), F32), jax.ShapeDtypeStruct((b, CONV_W - 1, D_MODEL), F32)],
        compiler_params=pltpu.CompilerParams(dimension_semantics=("arbitrary",), vmem_limit_bytes=VMEM_LIMIT),
        name="conv_streams",
    )(x.reshape(n, D_MODEL), state.reshape(b, CONV_W - 1, D_MODEL), gain, w_in, conv_w, w_out)
    return y.reshape(b, s, D_MODEL), ns.reshape(b, 1, CONV_W - 1, D_MODEL)


def _conv(x, state, gain, w_in, conv_w, w_out, *, tm, casts=()):
    b, s, _ = x.shape
    tm = min(tm, s)
    nsb = s // tm
    c_args, c_in, c_shapes, c_out = _cast_jobs(casts, b * nsb, lambda bi, i: bi * nsb + i)
    return pl.pallas_call(
        functools.partial(_conv_kernel, n_cast=len(casts)),
        grid=(b, nsb),
        in_specs=[
            pl.BlockSpec((1, tm, D_MODEL), lambda bi, i: (bi, i, 0)),
            pl.BlockSpec((1, 1, CONV_W - 1, D_MODEL), lambda bi, i: (bi, 0, 0, 0)),
            _resident(gain.shape), _resident(w_in.shape), _resident(conv_w.shape), _resident(w_out.shape),
        ] + c_in,
        out_specs=[
            pl.BlockSpec((1, tm, D_MODEL), lambda bi, i: (bi, i, 0)),
            pl.BlockSpec((1, 1, CONV_W - 1, D_MODEL), lambda bi, i: (bi, 0, 0, 0)),
        ] + c_out,
        out_shape=[jax.ShapeDtypeStruct((b, s, D_MODEL), F32),
                   jax.ShapeDtypeStruct((b, 1, CONV_W - 1, D_MODEL), F32)] + c_shapes,
        scratch_shapes=[pltpu.VMEM((8, D_MODEL), F32)],
        compiler_params=pltpu.CompilerParams(dimension_semantics=("arbitrary", "arbitrary"),
                                             vmem_limit_bytes=VMEM_LIMIT),
        name="conv",
    )(x, state, gain, w_in, conv_w, w_out, *c_args)


def _stack_q(q):
    lane = lax.broadcasted_iota(jnp.int32, q.shape, 1)
    zero = jnp.zeros_like(q)
    return jnp.concatenate([jnp.where(lane < HEAD_DIM, q, zero), jnp.where(lane >= HEAD_DIM, q, zero)], axis=0)


def _chunk_of(pos):
    return lax.shift_right_logical(pos, CHUNK.bit_length() - 1)


def _lambda(lq1, lk1, lq2, lk2):
    return (jnp.exp(jnp.sum(lq1 * lk1, axis=-1, keepdims=True))
            - jnp.exp(jnp.sum(lq2 * lk2, axis=-1, keepdims=True)) + LAMBDA_INIT)


def _finish(acc, l, t, lam, subln):
    o = acc[:t] / l[:t] - lam * (acc[t:] / l[t:])
    return (_rms(o, subln) * (1.0 - LAMBDA_INIT)).astype(BF16)


def _softmax_init(m_ref, l_ref, acc_ref):
    m_ref[...] = jnp.full(m_ref.shape, NEG, F32)
    l_ref[...] = jnp.zeros(l_ref.shape, F32)
    acc_ref[...] = jnp.zeros(acc_ref.shape, F32)


def _softmax_update(s, v, m_ref, l_ref, acc_ref):
    cols = [s[:, c * LANES:(c + 1) * LANES] for c in range(s.shape[1] // LANES)]
    m_prev = m_ref[...]
    m_next = jnp.maximum(m_prev, jnp.max(functools.reduce(jnp.maximum, cols), axis=1, keepdims=True))
    alpha = jnp.exp2(m_prev - m_next)
    ps = [jnp.exp2(c - m_next) for c in cols]
    l_ref[...] = alpha * l_ref[...] + functools.reduce(jnp.add, ps)
    p = jnp.concatenate([c.astype(BF16) for c in ps], axis=1)
    acc_ref[...] = alpha * acc_ref[...] + _dot(p, v)
    m_ref[...] = m_next


ONES_ROWS = 16


def _softmax_update_t(s, cmax, vt, m_ref, acc_ref):
    m_prev = m_ref[...]
    m_next = jnp.maximum(m_prev, cmax)
    alpha = jnp.exp2(m_prev - m_next)
    p = jnp.exp2(s - m_next).astype(BF16)
    vt1 = jnp.concatenate([vt, jnp.ones((ONES_ROWS, vt.shape[1]), BF16)], axis=0)
    acc_ref[...] = alpha * acc_ref[...] + _dot(vt1, p)
    m_ref[...] = m_next


SLOT_A, SLOT_B = 0, 1
FULL_UNROLLS = (8, 4)
DIAG_UNROLLS = (4, 2)


def _attn_prompt_kernel(fi_ref, fj_ref, qt_ref, k_ref, vt_ref, sub_ref, lq1_ref, lk1_ref, lq2_ref, lk2_ref, o_ref,
                        qs_sc, s_sc, cm_sc, m_sc, acc_sc, *, t):
    nq = qt_ref.shape[2] // t
    nfull = nq * (nq - 1) // 2
    feat = lax.broadcasted_iota(jnp.int32, (V_DIM, t), 0)
    for i in range(nq):
        qt = qt_ref[0, :, i * t:(i + 1) * t]
        zero = jnp.zeros_like(qt)
        qs_sc[i] = jnp.concatenate([jnp.where(feat < HEAD_DIM, qt, zero), jnp.where(feat >= HEAD_DIM, qt, zero)],
                                   axis=1)
    m_sc[...] = jnp.full(m_sc.shape, NEG, F32)
    acc_sc[...] = jnp.zeros(acc_sc.shape, F32)

    def S(i, j, slot, masked):
        kj = k_ref[0, pl.ds(pl.multiple_of(j * t, t), t), :]
        s = _dot(kj, qs_sc[i])
        if masked:
            key = lax.broadcasted_iota(jnp.int32, s.shape, 0)
            qry = lax.broadcasted_iota(jnp.int32, s.shape, 1)
            s = jnp.where(_chunk_of(key) <= _chunk_of(qry & (t - 1)), s, NEG)
        s_sc[slot] = s
        cm_sc[slot] = jnp.max(s, axis=0, keepdims=True)

    def C(i, j, slot):
        vtj = vt_ref[0, :, pl.ds(pl.multiple_of(j * t, t), t)]
        _softmax_update_t(s_sc[slot], cm_sc[slot], vtj, m_sc.at[i], acc_sc.at[i])

    s_full = lambda n, slot: S(fi_ref[n], fj_ref[n], slot, False)
    c_full = lambda n, slot: C(fi_ref[n], fj_ref[n], slot)
    s_diag = lambda d, slot: S(d, d, slot, True)
    c_diag = lambda d, slot: C(d, d, slot)

    if nfull:
        s_full(0, SLOT_A)
    else:
        s_diag(0, SLOT_A)
    slots = (SLOT_A, SLOT_B)
    done = 0
    for unroll in FULL_UNROLLS:
        trips = max(nfull - 1 - done, 0) // unroll

        def body(it, c, unroll=unroll, base=done):
            n = base + unroll * it
            for r in range(unroll):
                s_full(n + r + 1, slots[(r + 1) % 2])
                c_full(n + r, slots[r % 2])
            return c

        lax.fori_loop(0, trips, body, 0)
        done += unroll * trips
    for r, n in enumerate(range(done, nfull)):
        cur, nxt = slots[r % 2], slots[(r + 1) % 2]
        if n + 1 < nfull:
            s_full(n + 1, nxt)
        else:
            s_diag(0, nxt)
        c_full(n, cur)
    first, second = slots[(nfull - done) % 2], slots[(nfull - done + 1) % 2]

    slots = (first, second)
    done = 0
    for unroll in DIAG_UNROLLS:
        trips = max(nq - 1 - done, 0) // unroll

        def body(it, c, unroll=unroll, base=done):
            d = base + unroll * it
            for r in range(unroll):
                s_diag(d + r + 1, slots[(r + 1) % 2])
                c_diag(d + r, slots[r % 2])
            return c

        lax.fori_loop(0, trips, body, 0)
        done += unroll * trips
    for r, d in enumerate(range(done, nq)):
        if d + 1 < nq:
            s_diag(d + 1, slots[(r + 1) % 2])
        c_diag(d, slots[r % 2])

    lam = _lambda(lq1_ref[...], lk1_ref[...], lq2_ref[...], lk2_ref[...])
    gain = sub_ref[...] * (1.0 - LAMBDA_INIT)

    def finish(i, c):
        acc = acc_sc[i]
        on = acc[0:V_DIM, :] * (1.0 / acc[V_DIM:V_DIM + 1, :])
        o = on[:, :t] - lam * on[:, t:]
        inv = lax.rsqrt(jnp.mean(o * o, axis=0, keepdims=True) + EPS)
        o_ref[0, :, pl.ds(pl.multiple_of(i * t, t), t)] = (o * inv * gain).astype(BF16)
        return c

    lax.fori_loop(0, nq, finish, 0)


def _attn_prompt(qt, k, vt, subln, lq1, lk1, lq2, lk2, *, t):
    b, s, _ = k.shape
    nq = s // t
    pairs = [(i, j) for i in range(1, nq) for j in range(i)] or [(0, 0)]
    fi = jnp.asarray([p[0] for p in pairs], jnp.int32)
    fj = jnp.asarray([p[1] for p in pairs], jnp.int32)
    small = lambda a: pl.BlockSpec(a.shape, lambda bi, h, *_: (0,) * a.ndim, pipeline_mode=pl.Buffered(1))
    subln = subln.reshape(V_DIM, 1)
    rows = pl.BlockSpec((1, s, V_DIM), lambda bi, h, *_: (bi, 0, h))
    cols = pl.BlockSpec((1, V_DIM, s), lambda bi, h, *_: (bi, h, 0))
    return pl.pallas_call(
        functools.partial(_attn_prompt_kernel, t=t),
        grid_spec=pltpu.PrefetchScalarGridSpec(
            num_scalar_prefetch=2,
            grid=(b, N_HEADS),
            in_specs=[cols, rows, cols, small(subln), small(lq1), small(lk1), small(lq2), small(lk2)],
            out_specs=cols,
            scratch_shapes=[pltpu.VMEM((nq, V_DIM, 2 * t), BF16), pltpu.VMEM((2, t, 2 * t), F32),
                            pltpu.VMEM((2, 1, 2 * t), F32), pltpu.VMEM((nq, 1, 2 * t), F32),
                            pltpu.VMEM((nq, V_DIM + ONES_ROWS, 2 * t), F32)],
        ),
        out_shape=jax.ShapeDtypeStruct((b, D_MODEL, s), BF16),
        compiler_params=pltpu.CompilerParams(dimension_semantics=("arbitrary",) * 2, vmem_limit_bytes=VMEM_LIMIT),
        name="attn_prompt",
    )(fi, fj, qt, k, vt, subln, lq1, lk1, lq2, lk2)


def _attn_sample_kernel(q_ref, kc_ref, vc_ref, kn_ref, vn_ref, sub_ref, lq1_ref, lk1_ref, lq2_ref, lk2_ref,
                        o_ref, m_sc, l_sc, acc_sc, *, q_start):
    t = q_ref.shape[1]
    tkb = kc_ref.shape[1] // N_HEADS
    j = pl.program_id(1)

    @pl.when(j == 0)
    def _():
        _softmax_init(m_sc, l_sc, acc_sc)

    head_cols = lambda h: slice(h * V_DIM, (h + 1) * V_DIM)
    scores = []
    for h in range(N_HEADS):
        qs = _stack_q(q_ref[0, :, head_cols(h)])
        kh = kc_ref[0, pl.ds(h, tkb, stride=N_HEADS), :].astype(BF16)
        scores.append(_dot_nt(qs, kh))
    for h in range(N_HEADS):
        vh = vc_ref[0, pl.ds(h, tkb, stride=N_HEADS), :].astype(BF16)
        _softmax_update(scores[h], vh, m_sc.at[h], l_sc.at[h], acc_sc.at[h])

    @pl.when(j == pl.num_programs(1) - 1)
    def _():
        lam = _lambda(lq1_ref[...], lk1_ref[...], lq2_ref[...], lk2_ref[...])
        for h in range(N_HEADS):
            qs = _stack_q(q_ref[0, :, head_cols(h)])
            s_n = _dot_nt(qs, kn_ref[0, :, head_cols(h)])
            row = lax.broadcasted_iota(jnp.int32, s_n.shape, 0)
            col = lax.broadcasted_iota(jnp.int32, s_n.shape, 1)
            s_n = jnp.where(_chunk_of(q_start + col) <= _chunk_of(q_start + (row & (t - 1))), s_n, NEG)
            m_prev = m_sc[h]
            m_fin = jnp.maximum(m_prev, jnp.max(s_n, axis=1, keepdims=True))
            alpha = jnp.exp2(m_prev - m_fin)
            p_n = jnp.exp2(s_n - m_fin[:, :t])
            l = jnp.sum(alpha * l_sc[h], axis=1, keepdims=True) + jnp.sum(p_n, axis=1, keepdims=True)
            acc = alpha * acc_sc[h] + _dot(p_n.astype(BF16), vn_ref[0, :, head_cols(h)])
            o_ref[0, :, head_cols(h)] = _finish(acc, l, t, lam, sub_ref[...])


def _attn_sample(q, cache_k, cache_v, kn, vn, subln, lq1, lk1, lq2, lk2, *, tkb):
    b, t, _ = q.shape
    past = cache_k.shape[1]
    kc = cache_k.reshape(b, past * N_HEADS, V_DIM)
    vc = cache_v.reshape(b, past * N_HEADS, V_DIM)
    small = lambda a: _resident(a.shape)
    rows = pl.BlockSpec((1, t, D_MODEL), lambda bi, j: (bi, 0, 0))
    blk = pl.BlockSpec((1, tkb * N_HEADS, V_DIM), lambda bi, j: (bi, j, 0))
    return pl.pallas_call(
        functools.partial(_attn_sample_kernel, q_start=past),
        grid=(b, past // tkb),
        in_specs=[rows, blk, blk, rows, rows, small(subln), small(lq1), small(lk1), small(lq2), small(lk2)],
        out_specs=rows,
        out_shape=jax.ShapeDtypeStruct((b, t, D_MODEL), BF16),
        scratch_shapes=[pltpu.VMEM((N_HEADS, 2 * t, LANES), F32), pltpu.VMEM((N_HEADS, 2 * t, LANES), F32),
                        pltpu.VMEM((N_HEADS, 2 * t, V_DIM), F32)],
        compiler_params=pltpu.CompilerParams(dimension_semantics=("arbitrary",) * 2, vmem_limit_bytes=VMEM_LIMIT),
        name="attn_sample",
    )(q, kc, vc, kn, vn, subln, lq1, lk1, lq2, lk2)


def _trunk(x, conv_state, cache, w, bf, *, tm_ffn, tm_conv, t_attn):
    b, s, _ = x.shape
    n = b * s

    def with_casts(call, n_out, *names):
        todo = [name for name in names if name not in bf]
        outs = call([w["f32"][name] for name in todo])
        bf.update(zip(todo, outs[n_out:]))
        return outs[:n_out]

    (h,) = with_casts(lambda casts: _ffn(x.reshape(n, D_MODEL), w["g_ffn1"][0], bf["ffn_in00"], bf["ffn_out00"],
                                         tm=tm_ffn, casts=casts),
                      1, "conv_in", "conv_out", "ffn_in01", "ffn_out01")
    conv_args = (h.reshape(b, s, D_MODEL), conv_state, w["g_mix"][0], bf["conv_in"], w["conv_w"], bf["conv_out"])
    if s < tm_conv:
        h, new_conv = _conv_streams(*conv_args)
    else:
        h, new_conv = with_casts(lambda casts: _conv(*conv_args, tm=tm_conv, casts=casts),
                                 2, "w_q", "ffn_in10", "ffn_out10")
    x1, k_new, v_new, k_bf, v_bf = with_casts(
        lambda casts: _ffn(h.reshape(n, D_MODEL), w["g_ffn2"][0], bf["ffn_in01"], bf["ffn_out01"], tm=tm_ffn,
                           proj="kvt" if cache is None else "kv", proj_gain=w["g_kv"], proj_w=bf["w_kv"], seq=s,
                           casts=casts),
        5, "ffn_in11", "ffn_out11")
    h1, q_bf = with_casts(
        lambda casts: _ffn(x1, w["g_ffn1"][1], bf["ffn_in10"], bf["ffn_out10"], tm=tm_ffn,
                           proj="qt" if cache is None else "q", proj_gain=w["g_mix"][1], proj_w=bf["w_q"], seq=s,
                           casts=casts),
        2, "w_o")
    shp = (b, s, D_MODEL)
    lam = (w["lq1"], w["lk1"], w["lq2"], w["lk2"])
    if cache is None:
        o = _attn_prompt(q_bf, k_bf.reshape(shp), v_bf, w["subln"], *lam, t=t_attn)
    else:
        o = _attn_sample(q_bf.reshape(shp), cache[0], cache[1], k_bf.reshape(shp), v_bf.reshape(shp),
                         w["subln"], *lam, tkb=min(2048, cache[0].shape[1])).reshape(n, D_MODEL)
    (y,) = _ffn(h1, w["g_ffn2"][1], bf["ffn_in11"], bf["ffn_out11"], tm=tm_ffn, pre=(o, bf["w_o"]),
                final_gain=w["g_final"])
    kv_shape = (b, s, N_HEADS, V_DIM)
    return y.reshape(shp), new_conv, k_new.reshape(kv_shape), v_new.reshape(kv_shape)


def kernel(x_prompt, x_sample, state_conv, cache_k, cache_v, norm_ffn1, norm_mix, norm_ffn2, ffn_w_in, ffn_w_out,
           conv_w_in, conv_w, conv_w_out, norm_kv, w_k, w_v, w_q, lambda_q1, lambda_k1, lambda_q2, lambda_k2,
           subln, w_o, norm_final):
    depth = norm_ffn1.shape[0]
    row = lambda a: a.reshape(1, -1)
    w = {
        "g_ffn1": [row(norm_ffn1[i]) for i in range(depth)],
        "g_mix": [row(norm_mix[i]) for i in range(depth)],
        "g_ffn2": [row(norm_ffn2[i]) for i in range(depth)],
        "conv_w": conv_w[0],
        "g_kv": row(norm_kv),
        "lq1": row(lambda_q1[0]), "lk1": row(lambda_k1[0]), "lq2": row(lambda_q2[0]), "lk2": row(lambda_k2[0]),
        "subln": row(subln[0]),
        "g_final": row(norm_final),
        "f32": {
            "conv_in": (conv_w_in[0], ()), "conv_out": (conv_w_out[0], ()), "w_q": (w_q[0], ()), "w_o": (w_o[0], ()),
            "ffn_in01": (ffn_w_in, (0, 1)), "ffn_out01": (ffn_w_out, (0, 1)),
            "ffn_in10": (ffn_w_in, (1, 0)), "ffn_out10": (ffn_w_out, (1, 0)),
            "ffn_in11": (ffn_w_in, (1, 1)), "ffn_out11": (ffn_w_out, (1, 1)),
        },
    }
    ffn_in00, ffn_out00 = _cast_call([(ffn_w_in, (0, 0)), (ffn_w_out, (0, 0))])
    bf = {"ffn_in00": ffn_in00, "ffn_out00": ffn_out00,
          "w_kv": jnp.concatenate([w_k, w_v], axis=1).astype(BF16)}
    b = x_prompt.shape[0]
    conv0 = jnp.zeros((b, 1, CONV_W - 1, D_MODEL), x_prompt.dtype)
    y_p, conv_p, k_p, v_p = _trunk(x_prompt, conv0, None, w, bf, tm_ffn=512, tm_conv=512, t_attn=512)
    y_s, conv_s, k_s, v_s = _trunk(x_sample, state_conv, (cache_k, cache_v), w, bf, tm_ffn=512, tm_conv=512,
                                   t_attn=None)
    return (y_p, y_s, conv_p, k_p, v_p, conv_s, k_s, v_s)
```
